```python
import math
import jax
import jax.numpy as jnp
from jax import lax
import numpy as np

D_MODEL = 1024
BATCH = 8
SEQ = 2048
DEPTH = 4
DEC_BATCH = 128
DEC_SEQ = 4
PAST_LEN = 8192
PAGE_SIZE = 128

N_MIXERS = 2
N_ATTN_LAYERS = (DEPTH + 1) // 2
N_SSM_LAYERS = DEPTH // 2

N_HEADS = 8
Q_LORA = 384
KV_LORA = 256
QK_NOPE = 64
QK_ROPE = 32
QK_HEAD = QK_NOPE + QK_ROPE
V_HEAD = 64
ROPE_THETA = 10000.0
Q_BLOCK = 128

D_SSM = D_MODEL
SSM_GROUP = 16
N_GROUPS = D_SSM // SSM_GROUP
SSM_STATE = 64
DT_MIN = 0.001
DT_MAX = 0.1

D_FF = 2816
CONV_W = 3

NORM_EPS = 1e-6

kernel_name = 'mla_s5_convffn_hybrid_step'


def rmsnorm(x, g):
    xf = x.astype(jnp.float32)
    y = xf * lax.rsqrt(jnp.mean(xf * xf, axis=-1, keepdims=True) + NORM_EPS)
    return (y * g.astype(jnp.float32)).astype(x.dtype)


def rope_tables(pos):
    inv = ROPE_THETA ** (-jnp.arange(QK_ROPE // 2, dtype=jnp.float32) * (2.0 / QK_ROPE))
    ang = pos.astype(jnp.float32)[:, None] * inv[None, :]
    return jnp.cos(ang), jnp.sin(ang)


def apply_rope(x, cos, sin):
    half = x.shape[-1] // 2
    xf = x.astype(jnp.float32)
    x1, x2 = xf[..., :half], xf[..., half:]
    return jnp.concatenate([x1 * cos - x2 * sin, x2 * cos + x1 * sin], axis=-1).astype(x.dtype)


def mla_queries_latents(h, pos, w_in, g_qlat, w_q_up, g_kvlat, g_qn):
    proj = h @ w_in
    cq = proj[..., :Q_LORA]
    ckv = proj[..., Q_LORA:Q_LORA + KV_LORA]
    kpe = proj[..., Q_LORA + KV_LORA:]
    q = (rmsnorm(cq, g_qlat) @ w_q_up).reshape(h.shape[:-1] + (N_HEADS, QK_HEAD))
    cos, sin = rope_tables(pos)
    q = jnp.concatenate([q[..., :QK_NOPE],
                         apply_rope(q[..., QK_NOPE:], cos[:, None, :], sin[:, None, :])], axis=-1)
    q = rmsnorm(q, g_qn)
    return q, rmsnorm(ckv, g_kvlat), apply_rope(kpe, cos, sin)


def mla_keys_values(ckv, kpe, w_kv_up, g_kn):
    kv = (ckv @ w_kv_up).reshape(ckv.shape[:-1] + (N_HEADS, QK_NOPE + V_HEAD))
    k_nope, v = kv[..., :QK_NOPE], kv[..., QK_NOPE:]
    k_pe = jnp.broadcast_to(kpe[..., None, :].astype(k_nope.dtype), k_nope.shape[:-1] + (QK_ROPE,))
    k = rmsnorm(jnp.concatenate([k_nope, k_pe], axis=-1), g_kn)
    return k, v


def causal_attend(q, k, v, q_pos, k_pos):
    s = jnp.einsum('...qhd,...khd->...hqk', q, k).astype(jnp.float32) * (QK_HEAD ** -0.5)
    mask = k_pos[None, :] <= q_pos[:, None]
    s = jnp.where(mask, s, jnp.finfo(jnp.float32).min)
    p = jax.nn.softmax(s, axis=-1).astype(v.dtype)
    return jnp.einsum('...hqk,...khd->...qhd', p, v)


def mla_prompt(h, proj_w, kv_w, w_out):
    b, s, _ = h.shape
    pos = jnp.arange(s)
    q, ckv, kpe = mla_queries_latents(h, pos, *proj_w)
    k, v = mla_keys_values(ckv, kpe, *kv_w)
    nb = s // Q_BLOCK
    qb = jnp.moveaxis(q.reshape(b, nb, Q_BLOCK, N_HEADS, QK_HEAD), 1, 0)
    pb = pos.reshape(nb, Q_BLOCK)
    ob = lax.map(lambda a: causal_attend(a[0], k, v, a[1], pos), (qb, pb))
    o = jnp.moveaxis(ob, 0, 1).reshape(b, s, N_HEADS * V_HEAD)
    return o @ w_out, ckv, kpe


def mla_sample(h, ckv_pool, kpe_pool, page_table, proj_w, kv_w, w_out):
    b, t, _ = h.shape
    past = page_table.shape[1] * PAGE_SIZE
    q_pos = past + jnp.arange(t)
    k_pos = jnp.arange(past + t)
    q, ckv, kpe = mla_queries_latents(h, q_pos, *proj_w)

    def one_sequence(a):
        pt, q1, c1, r1 = a
        c_all = jnp.concatenate([ckv_pool[pt].reshape(past, KV_LORA).astype(c1.dtype), c1], axis=0)
        r_all = jnp.concatenate([kpe_pool[pt].reshape(past, QK_ROPE).astype(r1.dtype), r1], axis=0)
        k, v = mla_keys_values(c_all, r_all, *kv_w)
        return causal_attend(q1, k, v, q_pos, k_pos)

    o = lax.map(one_sequence, (page_table, q, ckv, kpe)).reshape(b, t, N_HEADS * V_HEAD)
    return o @ w_out, ckv, kpe


def s5_mixer(h, s0_re, s0_im, w_in, log_dt, a_re, a_im, b_re, b_im, c_re, c_im, d_skip, w_out):
    f32 = jnp.float32
    bsz, t, _ = h.shape
    u = (h @ w_in).astype(f32).reshape(bsz, t, N_GROUPS, SSM_GROUP)
    dt = jnp.exp(log_dt.astype(f32))
    ar = a_re.astype(f32)
    ai = a_im.astype(f32)
    mag = jnp.exp(ar * dt)
    lr = mag * jnp.cos(ai * dt)
    li = mag * jnp.sin(ai * dt)
    den = ar * ar + ai * ai
    cr = ((lr - 1.0) * ar + li * ai) / den
    ci = (li * ar - (lr - 1.0) * ai) / den
    br = b_re.astype(f32)
    bi = b_im.astype(f32)
    bbr = cr[..., None] * br - ci[..., None] * bi
    bbi = cr[..., None] * bi + ci[..., None] * br
    xr = jnp.einsum('btgc,gpc->btgp', u, bbr)
    xi = jnp.einsum('btgc,gpc->btgp', u, bbi)
    s0r = s0_re.astype(f32)
    s0i = s0_im.astype(f32)
    xr = xr.at[:, 0].add(lr * s0r - li * s0i)
    xi = xi.at[:, 0].add(lr * s0i + li * s0r)
    a_r = jnp.broadcast_to(lr, xr.shape)
    a_i = jnp.broadcast_to(li, xi.shape)

    def combine(e1, e2):
        a1r, a1i, b1r, b1i = e1
        a2r, a2i, b2r, b2i = e2
        return (a1r * a2r - a1i * a2i, a1r * a2i + a1i * a2r,
                a2r * b1r - a2i * b1i + b2r, a2r * b1i + a2i * b1r + b2i)

    _, _, sr, si = lax.associative_scan(combine, (a_r, a_i, xr, xi), axis=1)
    y = (jnp.einsum('gcp,btgp->btgc', c_re.astype(f32), sr)
         - jnp.einsum('gcp,btgp->btgc', c_im.astype(f32), si)
         + d_skip.astype(f32).reshape(N_GROUPS, SSM_GROUP) * u)
    y = jax.nn.gelu(y.reshape(bsz, t, D_SSM)).astype(h.dtype)
    g = y @ w_out
    out = g[..., :D_MODEL] * jax.nn.sigmoid(g[..., D_MODEL:])
    return out, sr[:, -1].astype(s0_re.dtype), si[:, -1].astype(s0_im.dtype)


def conv_ffn(h, prev, w_up, conv_w, conv_b, w_down):
    u = h @ w_up
    t = u.shape[1]
    xp = jnp.concatenate([prev.astype(u.dtype), u], axis=1)
    c = conv_b
    for j in range(CONV_W):
        c = c + conv_w[j] * xp[:, j:j + t]
    val, gate = c[..., :D_FF], c[..., D_FF:]
    return (jax.nn.silu(gate) * val) @ w_down, xp[:, t:]


def setup_inputs(seed: int = 0) -> dict:
    key = jax.random.key(seed)
    ks = iter(jax.random.split(key, 48))
    f32 = jnp.float32

    def nrm(shape, scale):
        return scale * jax.random.normal(next(ks), shape, f32)

    def gain(shape):
        return 1.0 + 0.05 * jax.random.normal(next(ks), shape, f32)

    n_pages = PAST_LEN // PAGE_SIZE
    n_used = DEC_BATCH * n_pages
    n_pool = n_used + max(1, n_used // 4)
    na, ns = N_ATTN_LAYERS, N_SSM_LAYERS

    x_prompt = nrm((BATCH, SEQ, D_MODEL), 1.0)
    x_sample = nrm((DEC_BATCH, DEC_SEQ, D_MODEL), 1.0)
    cache_ckv = nrm((na, n_pool, PAGE_SIZE, KV_LORA), 1.0)
    cache_kpe = nrm((na, n_pool, PAGE_SIZE, QK_ROPE), 1.0)
    state_ssm_re = nrm((ns, DEC_BATCH, N_GROUPS, SSM_STATE), 0.5)
    state_ssm_im = nrm((ns, DEC_BATCH, N_GROUPS, SSM_STATE), 0.5)
    state_conv = nrm((DEPTH, DEC_BATCH, CONV_W - 1, 2 * D_FF), 1.0)
    page_table = jax.random.permutation(next(ks), n_pool)[:n_used].reshape(DEC_BATCH, n_pages).astype(jnp.int32)

    attn_norm = gain((na, D_MODEL))
    attn_w_in = nrm((na, D_MODEL, Q_LORA + KV_LORA + QK_ROPE), D_MODEL ** -0.5)
    attn_q_lat_norm = gain((na, Q_LORA))
    attn_w_q_up = nrm((na, Q_LORA, N_HEADS * QK_HEAD), Q_LORA ** -0.5)
    attn_kv_lat_norm = gain((na, KV_LORA))
    attn_w_kv_up = nrm((na, KV_LORA, N_HEADS * (QK_NOPE + V_HEAD)), KV_LORA ** -0.5)
    attn_q_norm = gain((na, QK_HEAD))
    attn_k_norm = gain((na, QK_HEAD))
    attn_w_out = nrm((na, N_HEADS * V_HEAD, D_MODEL), (N_HEADS * V_HEAD) ** -0.5)

    ssm_norm = gain((ns, D_MODEL))
    ssm_w_in = nrm((ns, D_MODEL, D_SSM), D_MODEL ** -0.5)
    ssm_log_dt = jax.random.uniform(next(ks), (ns, N_GROUPS, SSM_STATE), f32,
                                    minval=math.log(DT_MIN), maxval=math.log(DT_MAX))
    ssm_a_re = -0.5 + nrm((ns, N_GROUPS, SSM_STATE), 0.01)
    ssm_a_im = math.pi * jnp.arange(SSM_STATE, dtype=f32) + nrm((ns, N_GROUPS, SSM_STATE), 0.01)
    ssm_b_re = nrm((ns, N_GROUPS, SSM_STATE, SSM_GROUP), (2 * SSM_GROUP) ** -0.5)
    ssm_b_im = nrm((ns, N_GROUPS, SSM_STATE, SSM_GROUP), (2 * SSM_GROUP) ** -0.5)
    ssm_c_re = nrm((ns, N_GROUPS, SSM_GROUP, SSM_STATE), SSM_STATE ** -0.5)
    ssm_c_im = nrm((ns, N_GROUPS, SSM_GROUP, SSM_STATE), SSM_STATE ** -0.5)
    ssm_d = nrm((ns, D_SSM), 1.0)
    ssm_w_out = nrm((ns, D_SSM, 2 * D_MODEL), D_SSM ** -0.5)

    ffn_norm = gain((DEPTH, D_MODEL))
    ffn_w_up = nrm((DEPTH, D_MODEL, 2 * D_FF), D_MODEL ** -0.5)
    ffn_conv_w = nrm((DEPTH, CONV_W, 2 * D_FF), CONV_W ** -0.5)
    ffn_conv_b = nrm((DEPTH, 2 * D_FF), 0.01)
    ffn_w_down = nrm((DEPTH, D_FF, D_MODEL), D_FF ** -0.5)

    return {
        'x_prompt': x_prompt, 'x_sample': x_sample,
        'cache_ckv': cache_ckv, 'cache_kpe': cache_kpe,
        'state_ssm_re': state_ssm_re, 'state_ssm_im': state_ssm_im,
        'state_conv': state_conv, 'page_table': page_table,
        'attn_norm': attn_norm, 'attn_w_in': attn_w_in, 'attn_q_lat_norm': attn_q_lat_norm,
        'attn_w_q_up': attn_w_q_up, 'attn_kv_lat_norm': attn_kv_lat_norm, 'attn_w_kv_up': attn_w_kv_up,
        'attn_q_norm': attn_q_norm, 'attn_k_norm': attn_k_norm, 'attn_w_out': attn_w_out,
        'ssm_norm': ssm_norm, 'ssm_w_in': ssm_w_in, 'ssm_log_dt': ssm_log_dt,
        'ssm_a_re': ssm_a_re, 'ssm_a_im': ssm_a_im, 'ssm_b_re': ssm_b_re, 'ssm_b_im': ssm_b_im,
        'ssm_c_re': ssm_c_re, 'ssm_c_im': ssm_c_im, 'ssm_d': ssm_d, 'ssm_w_out': ssm_w_out,
        'ffn_norm': ffn_norm, 'ffn_w_up': ffn_w_up, 'ffn_conv_w': ffn_conv_w,
        'ffn_conv_b': ffn_conv_b, 'ffn_w_down': ffn_w_down,
    }


def reference(x_prompt, x_sample, cache_ckv, cache_kpe, state_ssm_re, state_ssm_im, state_conv, page_table,
              attn_norm, attn_w_in, attn_q_lat_norm, attn_w_q_up, attn_kv_lat_norm, attn_w_kv_up,
              attn_q_norm, attn_k_norm, attn_w_out,
              ssm_norm, ssm_w_in, ssm_log_dt, ssm_a_re, ssm_a_im, ssm_b_re, ssm_b_im,
              ssm_c_re, ssm_c_im, ssm_d, ssm_w_out,
              ffn_norm, ffn_w_up, ffn_conv_w, ffn_conv_b, ffn_w_down):
    xp, xs = x_prompt, x_sample
    bp = xp.shape[0]
    conv_zero = jnp.zeros((bp, CONV_W - 1, 2 * D_FF), xp.dtype)
    ssm_zero = jnp.zeros((bp, N_GROUPS, SSM_STATE), jnp.float32)
    ckv_p, kpe_p, ckv_s, kpe_s = [], [], [], []
    sre_p, sim_p, sre_s, sim_s = [], [], [], []
    conv_p, conv_s = [], []
    for i in range(DEPTH):
        j = i // N_MIXERS
        if i % N_MIXERS == 0:
            proj_w = (attn_w_in[j], attn_q_lat_norm[j], attn_w_q_up[j], attn_kv_lat_norm[j], attn_q_norm[j])
            kv_w = (attn_w_kv_up[j], attn_k_norm[j])
            mp, c_p, r_p = mla_prompt(rmsnorm(xp, attn_norm[j]), proj_w, kv_w, attn_w_out[j])
            ms, c_s, r_s = mla_sample(rmsnorm(xs, attn_norm[j]), cache_ckv[j], cache_kpe[j], page_table,
                                      proj_w, kv_w, attn_w_out[j])
            ckv_p.append(c_p)
            kpe_p.append(r_p)
            ckv_s.append(c_s)
            kpe_s.append(r_s)
        else:
            ssm_w = (ssm_w_in[j], ssm_log_dt[j], ssm_a_re[j], ssm_a_im[j], ssm_b_re[j], ssm_b_im[j],
                     ssm_c_re[j], ssm_c_im[j], ssm_d[j], ssm_w_out[j])
            mp, r_p, i_p = s5_mixer(rmsnorm(xp, ssm_norm[j]), ssm_zero, ssm_zero, *ssm_w)
            ms, r_s, i_s = s5_mixer(rmsnorm(xs, ssm_norm[j]), state_ssm_re[j], state_ssm_im[j], *ssm_w)
            sre_p.append(r_p)
            sim_p.append(i_p)
            sre_s.append(r_s)
            sim_s.append(i_s)
        xp = xp + mp
        xs = xs + ms
        ffn_w = (ffn_w_up[i], ffn_conv_w[i], ffn_conv_b[i], ffn_w_down[i])
        fp, cv_p = conv_ffn(rmsnorm(xp, ffn_norm[i]), conv_zero, *ffn_w)
        fs, cv_s = conv_ffn(rmsnorm(xs, ffn_norm[i]), state_conv[i], *ffn_w)
        conv_p.append(cv_p)
        conv_s.append(cv_s)
        xp = xp + fp
        xs = xs + fs
    return (xp, xs,
            jnp.stack(ckv_p), jnp.stack(kpe_p), jnp.stack(sre_p), jnp.stack(sim_p), jnp.stack(conv_p),
            jnp.stack(ckv_s), jnp.stack(kpe_s), jnp.stack(sre_s), jnp.stack(sim_s), jnp.stack(conv_s))
```

```python
import functools
import math

import jax
import jax.numpy as jnp
import numpy as np
from jax import lax
from jax.experimental import pallas as pl
from jax.experimental.pallas import tpu as pltpu

LANES = 128
SUBLANES = 8
VMEM_LIMIT_BYTES = 56 * 1024 * 1024

N_HEADS = 8
QK_NOPE = 64
QK_ROPE = 32
QK_HEAD = QK_NOPE + QK_ROPE
V_HEAD = 64
HEAD_PAD = LANES
QK_PAD = N_HEADS * HEAD_PAD
Q_LORA = 384
KV_LORA = 256
PAGE_SIZE = 128
ROPE_THETA = 10000.0
SSM_GROUP = 16
SSM_STATE = 64
CONV_W = 3
NORM_EPS = 1e-6

BF16 = jnp.bfloat16
F32 = jnp.float32


def _params(*sem):
    return pltpu.CompilerParams(dimension_semantics=sem, vmem_limit_bytes=VMEM_LIMIT_BYTES)


def _dot(a, b):
    return jnp.dot(a, b, preferred_element_type=F32)


def _dot_nt(a, b):
    return lax.dot_general(a, b, (((1,), (1,)), ((), ())), preferred_element_type=F32)


def _dot_tn(a, b):
    return lax.dot_general(a, b, (((0,), (0,)), ((), ())), preferred_element_type=F32)


def _rms(x, g):
    return x * lax.rsqrt(jnp.mean(x * x, axis=-1, keepdims=True) + NORM_EPS) * g


def _const_spec(shape):
    n = len(shape)
    return pl.BlockSpec(shape, lambda *_: (0,) * n)


def _ffn_kernel(x_ref, prev_ref, g_ref, wup_ref, cw_ref, cb_ref, wdn_ref, o_ref, st_ref, u_ref, a_ref,
                *, nb, tm, off, d_ff, cchunk):
    i = pl.program_id(1)
    hist = (CONV_W - 1) * nb

    @pl.when(i == 0)
    def _():
        u_ref[off - hist:off, :] = prev_ref[...]

    @pl.when(i > 0)
    def _():
        u_ref[off - hist:off, :] = u_ref[off + tm - hist:off + tm, :]

    x = x_ref[...]
    h = _rms(x, g_ref[...]).astype(BF16)
    u_ref[off:off + tm, :] = _dot(h, wup_ref[...])
    st_ref[...] = u_ref[off + tm - hist:off + tm, :]

    for c0 in range(0, d_ff, cchunk):
        def conv(lo):
            c = cb_ref[:, lo:lo + cchunk]
            for j in range(CONV_W):
                r0 = off - (CONV_W - 1 - j) * nb
                c = c + cw_ref[j:j + 1, lo:lo + cchunk] * u_ref[r0:r0 + tm, lo:lo + cchunk]
            return c
        val = conv(c0)
        gate = conv(d_ff + c0)
        a_ref[:, c0:c0 + cchunk] = (jax.nn.silu(gate) * val).astype(BF16)

    o_ref[...] = x + _dot(a_ref[...], wdn_ref[...])


def _ffn_call(x, prev, g, w_up, conv_w, conv_b, w_down, *, nb, tm):
    nseq, rows, d = x.shape
    d_ff = w_down.shape[0]
    hist = (CONV_W - 1) * nb
    off = -(-hist // SUBLANES) * SUBLANES
    cchunk = LANES
    assert rows % tm == 0 and tm % nb == 0 and d_ff % cchunk == 0
    kern = functools.partial(_ffn_kernel, nb=nb, tm=tm, off=off, d_ff=d_ff, cchunk=cchunk)
    return pl.pallas_call(
        kern,
        grid=(nseq, rows // tm),
        in_specs=[
            pl.BlockSpec((None, tm, d), lambda s, i: (s, i, 0)),
            pl.BlockSpec((None, hist, 2 * d_ff), lambda s, i: (s, 0, 0)),
            _const_spec((1, d)),
            _const_spec((d, 2 * d_ff)),
            _const_spec((CONV_W, 2 * d_ff)),
            _const_spec((1, 2 * d_ff)),
            _const_spec((d_ff, d)),
        ],
        out_specs=[
            pl.BlockSpec((None, tm, d), lambda s, i: (s, i, 0)),
            pl.BlockSpec((None, hist, 2 * d_ff), lambda s, i: (s, 0, 0)),
        ],
        out_shape=[
            jax.ShapeDtypeStruct((nseq, rows, d), F32),
            jax.ShapeDtypeStruct((nseq, hist, 2 * d_ff), F32),
        ],
        scratch_shapes=[
            pltpu.VMEM((off + tm, 2 * d_ff), F32),
            pltpu.VMEM((tm, d_ff), BF16),
        ],
        compiler_params=_params("arbitrary", "arbitrary"),
        name="conv_ffn",
    )(x, prev, g.reshape(1, d), w_up, conv_w, conv_b.reshape(1, 2 * d_ff), w_down)


def _rope128(x, c, sa, sb):
    n = x.shape[-1]
    return x * c + pltpu.roll(x, n - QK_ROPE // 2, 1) * sa + pltpu.roll(x, QK_ROPE // 2, 1) * sb


def _head_ms(x, ones2):
    sq = (x * x).astype(BF16)
    parts = [_dot(sq[:, c0:c0 + 2 * HEAD_PAD], ones2) for c0 in range(0, x.shape[-1], 2 * HEAD_PAD)]
    return jnp.concatenate(parts, axis=-1) * (1.0 / QK_HEAD)


def _mla_proj_kernel(x_ref, tc_ref, tsa_ref, tsb_ref, gin_ref, win_ref, gql_ref, wq_ref, gkvl_ref, wkv_ref,
                     gq_ref, gk_ref, ones2_ref,
                     q_ref, k_ref, v_ref, ckv_ref, kpe_ref):
    x = x_ref[...]
    h = _rms(x, gin_ref[...]).astype(BF16)
    proj = _dot(h, win_ref[...])
    cq = proj[:, :Q_LORA]
    ckv = proj[:, Q_LORA:Q_LORA + KV_LORA]
    kp = proj[:, Q_LORA + KV_LORA:]
    tc, tsa, tsb = tc_ref[...], tsa_ref[...], tsb_ref[...]
    ones2 = ones2_ref[...]

    q = _dot(_rms(cq, gql_ref[...]).astype(BF16), wq_ref[...])
    rep = lambda t: jnp.concatenate([t] * N_HEADS, axis=-1)
    q = _rope128(q, rep(tc), rep(tsa), rep(tsb))
    q = q * lax.rsqrt(_head_ms(q, ones2) + NORM_EPS) * gq_ref[...]
    q_ref[...] = q.astype(q_ref.dtype)

    ckv_n = _rms(ckv, gkvl_ref[...])
    ckv_ref[...] = ckv_n
    kp = _rope128(kp, tc, tsa, tsb)
    kpe_ref[...] = kp[:, QK_NOPE:QK_HEAD]

    kv = _dot(ckv_n.astype(BF16), wkv_ref[...])
    k = kv[:, :QK_PAD] + rep(kp)
    k = k * lax.rsqrt(_head_ms(k, ones2) + NORM_EPS) * gk_ref[...]
    k_ref[...] = k.astype(k_ref.dtype)
    v_ref[...] = kv[:, QK_PAD:].astype(v_ref.dtype)


def _mla_proj_call(x, tabs, w, *, tm):
    nseq, rows, d = x.shape
    assert rows % tm == 0
    row_spec = lambda n: pl.BlockSpec((None, tm, n), lambda s, i: (s, i, 0))
    tab_spec = pl.BlockSpec((tm, LANES), lambda s, i: (i, 0))
    consts = [w['g_in'], w['w_in'], w['g_ql'], w['w_q'], w['g_kvl'], w['w_kv'], w['g_q'], w['g_k'], w['ones2']]
    return pl.pallas_call(
        _mla_proj_kernel,
        grid=(nseq, rows // tm),
        in_specs=[row_spec(d), tab_spec, tab_spec, tab_spec] + [_const_spec(c.shape) for c in consts],
        out_specs=[row_spec(QK_PAD), row_spec(QK_PAD), row_spec(QK_PAD), row_spec(KV_LORA), row_spec(QK_ROPE)],
        out_shape=[
            jax.ShapeDtypeStruct((nseq, rows, QK_PAD), BF16),
            jax.ShapeDtypeStruct((nseq, rows, QK_PAD), BF16),
            jax.ShapeDtypeStruct((nseq, rows, QK_PAD), BF16),
            jax.ShapeDtypeStruct((nseq, rows, KV_LORA), F32),
            jax.ShapeDtypeStruct((nseq, rows, QK_ROPE), F32),
        ],
        compiler_params=_params("arbitrary", "arbitrary"),
        name="mla_proj",
    )(x, *tabs, *consts)


def _head_pad_cols(w, per_head, lo, hi):
    k = w.shape[0]
    w = w.reshape(k, N_HEADS, per_head)[:, :, lo:hi]
    w = jnp.pad(w, ((0, 0), (0, 0), (0, HEAD_PAD - (hi - lo))))
    return w.reshape(k, QK_PAD)


def _head_pad_vec(g):
    return jnp.tile(jnp.pad(g, (0, HEAD_PAD - g.shape[0])), N_HEADS).reshape(1, QK_PAD)


def _rope_tables(pos):
    half = QK_ROPE // 2
    inv = ROPE_THETA ** (-jnp.arange(half, dtype=F32) * (2.0 / QK_ROPE))
    ang = pos.astype(F32)[:, None] * inv[None, :]
    cos, sin = jnp.cos(ang), jnp.sin(ang)
    t = pos.shape[0]
    z = lambda n: jnp.zeros((t, n), F32)
    c = jnp.concatenate([jnp.ones((t, QK_NOPE), F32), cos, cos, z(HEAD_PAD - QK_HEAD)], axis=-1)
    sa = jnp.concatenate([z(QK_NOPE), -sin, z(half), z(HEAD_PAD - QK_HEAD)], axis=-1)
    sb = jnp.concatenate([z(QK_NOPE), z(half), sin, z(HEAD_PAD - QK_HEAD)], axis=-1)
    return c, sa, sb


def _mla_weights(g_in, w_in, g_ql, w_q_up, g_kvl, w_kv_up, g_qn, g_kn, w_out):
    d = w_in.shape[0]
    w_kpe = jnp.pad(w_in[:, Q_LORA + KV_LORA:], ((0, 0), (QK_NOPE, HEAD_PAD - QK_HEAD)))
    w_in_ext = jnp.concatenate([w_in[:, :Q_LORA + KV_LORA], w_kpe], axis=1)
    w_uk = _head_pad_cols(w_kv_up, QK_NOPE + V_HEAD, 0, QK_NOPE)
    w_uv = _head_pad_cols(w_kv_up, QK_NOPE + V_HEAD, QK_NOPE, QK_NOPE + V_HEAD)
    blk = np.kron(np.eye(2, dtype=np.float32), np.ones((HEAD_PAD, HEAD_PAD), np.float32))
    w_out_pad = jnp.pad(w_out.reshape(N_HEADS, V_HEAD, d), ((0, 0), (0, HEAD_PAD - V_HEAD), (0, 0)))
    return {
        'g_in': g_in.reshape(1, d),
        'w_in': w_in_ext.astype(BF16),
        'g_ql': g_ql.reshape(1, Q_LORA),
        'w_q': _head_pad_cols(w_q_up, QK_HEAD, 0, QK_HEAD).astype(BF16),
        'g_kvl': g_kvl.reshape(1, KV_LORA),
        'w_kv': jnp.concatenate([w_uk, w_uv], axis=1).astype(BF16),
        'g_q': _head_pad_vec(g_qn) * (QK_HEAD ** -0.5),
        'g_k': _head_pad_vec(g_kn),
        'ones2': jnp.asarray(blk, BF16),
        'w_out': w_out_pad.reshape(QK_PAD, d).astype(BF16),
        'w_uk': w_uk, 'w_uv': w_uv, 'g_kn': g_kn,
    }


def _flash_kernel(q_ref, k_ref, v_ref, x_ref, wo_ref, o_ref, m_ref, l_ref, acc_ref, *, tq, tk):
    i = pl.program_id(1)
    j = pl.program_id(2)

    @pl.when(j == 0)
    def _():
        m_ref[...] = jnp.full(m_ref.shape, -jnp.inf, F32)
        l_ref[...] = jnp.zeros(l_ref.shape, F32)
        acc_ref[...] = jnp.zeros(acc_ref.shape, F32)

    def step(masked):
        if masked:
            row = i * tq + lax.broadcasted_iota(jnp.int32, (tq, tk), 0)
            col = j * tk + lax.broadcasted_iota(jnp.int32, (tq, tk), 1)
            keep = col <= row
        for h in range(N_HEADS):
            hs = slice(h * HEAD_PAD, (h + 1) * HEAD_PAD)
            s = _dot_nt(q_ref[:, hs], k_ref[:, hs])
            if masked:
                s = jnp.where(keep, s, jnp.finfo(F32).min)
            m_old = m_ref[h]
            m_new = jnp.maximum(m_old, jnp.max(s, axis=-1, keepdims=True))
            alpha = jnp.exp(m_old - m_new)
            p = jnp.exp(s - m_new[:, :1])
            l_ref[h] = alpha * l_ref[h] + jnp.sum(p, axis=-1, keepdims=True)
            acc_ref[h] = alpha * acc_ref[h] + _dot(p.astype(BF16), v_ref[:, hs])
            m_ref[h] = m_new

    last = (i * tq + tq - 1) // tk

    @pl.when(j < last)
    def _():
        step(False)

    @pl.when(j == last)
    def _():
        step(True)
        o = jnp.concatenate([acc_ref[h] / l_ref[h] for h in range(N_HEADS)], axis=-1)
        o_ref[...] = x_ref[...] + _dot(o.astype(BF16), wo_ref[...])


def _flash_call(q, k, v, x, w_out_pad, *, tq, tk):
    nseq, s, d = x.shape
    assert s % tq == 0 and s % tk == 0
    kern = functools.partial(_flash_kernel, tq=tq, tk=tk)
    kv_map = lambda b, i, j: (b, jnp.minimum(j, (i * tq + tq - 1) // tk), 0)
    return pl.pallas_call(
        kern,
        grid=(nseq, s // tq, s // tk),
        in_specs=[
            pl.BlockSpec((None, tq, QK_PAD), lambda b, i, j: (b, i, 0)),
            pl.BlockSpec((None, tk, QK_PAD), kv_map),
            pl.BlockSpec((None, tk, QK_PAD), kv_map),
            pl.BlockSpec((None, tq, d), lambda b, i, j: (b, i, 0)),
            _const_spec((QK_PAD, d)),
        ],
        out_specs=pl.BlockSpec((None, tq, d), lambda b, i, j: (b, i, 0)),
        out_shape=jax.ShapeDtypeStruct((nseq, s, d), F32),
        scratch_shapes=[
            pltpu.VMEM((N_HEADS, tq, HEAD_PAD), F32),
            pltpu.VMEM((N_HEADS, tq, HEAD_PAD), F32),
            pltpu.VMEM((N_HEADS, tq, HEAD_PAD), F32),
        ],
        compiler_params=_params("arbitrary", "arbitrary", "arbitrary"),
        name="mla_flash",
    )(q, k, v, x, w_out_pad)


NEW_PAD = 16
KEY_TILE = 1024


def _paged_kernel(pt_ref, q_ref, cnew_ref, rnew_ref, ckv_hbm, kpe_hbm, wukt_ref, wuk_ref, seg_ref, onespe_ref,
                  gkr_ref, wuv_ref, o_ref, cbuf, rbuf, s_ref, cbf_ref, csem, rsem, *, layer, n_pages, n_new):
    b = pl.program_id(0)
    nb = pl.num_programs(0)
    slot = b % 2
    past = n_pages * PAGE_SIZE
    ncol = s_ref.shape[-1]

    def copies(bb, sl, p):
        pg = pt_ref[bb, p]
        rows = pl.ds(pl.multiple_of(p * PAGE_SIZE, PAGE_SIZE), PAGE_SIZE)
        return (pltpu.make_async_copy(ckv_hbm.at[layer, pg], cbuf.at[sl, rows, :], csem.at[sl]),
                pltpu.make_async_copy(kpe_hbm.at[layer, pg], rbuf.at[sl, rows, :], rsem.at[sl]))

    def start_all(bb, sl):
        def body(p, c):
            for cp in copies(bb, sl, p):
                cp.start()
            return c
        lax.fori_loop(0, n_pages, body, 0)

    @pl.when(b == 0)
    def _():
        start_all(0, 0)

    @pl.when(b + 1 < nb)
    def _():
        start_all(b + 1, 1 - slot)

    def wait_body(p, c):
        for cp in copies(b, slot, p):
            cp.wait()
        return c
    lax.fori_loop(0, n_pages, wait_body, 0)

    nq = n_new * N_HEADS
    q = q_ref[...]
    qb = jnp.concatenate([jnp.broadcast_to(q[t:t + 1], (N_HEADS, QK_PAD)) for t in range(n_new)], axis=0)
    row_head = lax.broadcasted_iota(jnp.int32, (nq, QK_PAD), 0) % N_HEADS
    lane_head = lax.broadcasted_iota(jnp.int32, (nq, QK_PAD), 1) // HEAD_PAD
    own_head = row_head == lane_head
    qb = jnp.where(own_head, qb, 0.0)
    qb = jnp.concatenate([qb, jnp.zeros((ncol - nq, QK_PAD), F32)], axis=0)
    qlat = _dot(qb.astype(BF16), wukt_ref[...]).astype(BF16)
    qsum = qb[:, :HEAD_PAD]
    for h in range(1, N_HEADS):
        qsum = qsum + qb[:, h * HEAD_PAD:(h + 1) * HEAD_PAD]
    qpe = (qsum[:, QK_NOPE:QK_HEAD] * gkr_ref[...]).astype(BF16)

    def scores(ckv_t, kpe_t):
        cb = ckv_t.astype(BF16)
        kn = _dot(cb, wuk_ref[...])
        ssq = _dot((kn * kn).astype(BF16), seg_ref[...]) + _dot((kpe_t * kpe_t).astype(BF16), onespe_ref[...])
        s = _dot_nt(cb, qlat) + _dot_nt(kpe_t.astype(BF16), qpe)
        return s * lax.rsqrt(ssq * (1.0 / QK_HEAD) + NORM_EPS), cb

    def tile_rows(i):
        return pl.ds(pl.multiple_of(i * KEY_TILE, KEY_TILE), KEY_TILE)

    def pass1(i, m):
        rows = tile_rows(i)
        s, cb = scores(cbuf[slot, rows, :], rbuf[slot, rows, :])
        s_ref[rows, :] = s
        cbf_ref[rows, :] = cb
        return jnp.maximum(m, jnp.max(s, axis=0, keepdims=True))

    m = lax.fori_loop(0, past // KEY_TILE, pass1, jnp.full((1, ncol), -jnp.inf, F32))
    s_new, cb_new = scores(cnew_ref[...], rnew_ref[...])
    key = lax.broadcasted_iota(jnp.int32, (NEW_PAD, ncol), 0)
    qt = lax.broadcasted_iota(jnp.int32, (NEW_PAD, ncol), 1) // N_HEADS
    s_new = jnp.where((key < n_new) & (key <= qt), s_new, jnp.finfo(F32).min)
    m = jnp.maximum(m, jnp.max(s_new, axis=0, keepdims=True))
    new_rows = slice(past, past + NEW_PAD)
    cbf_ref[new_rows, :] = cb_new

    def pass2(i, l):
        rows = tile_rows(i)
        p = jnp.exp(s_ref[rows, :] - m)
        s_ref[rows, :] = p
        return l + jnp.sum(p, axis=0, keepdims=True)

    l = lax.fori_loop(0, past // KEY_TILE, pass2, jnp.zeros((1, ncol), F32))
    p_new = jnp.exp(s_new - m)
    l = l + jnp.sum(p_new, axis=0, keepdims=True)
    inv = 1.0 / l

    def pass3(i, acc):
        rows = tile_rows(i)
        return acc + _dot_tn((s_ref[rows, :] * inv).astype(BF16), cbf_ref[rows, :])

    o_lat = _dot_tn((p_new * inv).astype(BF16), cbf_ref[new_rows, :])
    o_lat = lax.fori_loop(0, past // KEY_TILE, pass3, o_lat)

    o = _dot(o_lat[:nq].astype(BF16), wuv_ref[...])
    o = jnp.where(own_head, o, 0.0)
    o_ref[...] = jnp.sum(o.reshape(n_new, N_HEADS, QK_PAD), axis=1)


def _paged_call(page_table, q, ckv_new, kpe_new, cache_ckv, cache_kpe, w, *, layer):
    nb, n_new, _ = q.shape
    n_pages = page_table.shape[1]
    past = n_pages * PAGE_SIZE
    assert past % KEY_TILE == 0 and n_new <= NEW_PAD
    ncol = LANES
    assert n_new * N_HEADS <= ncol
    kern = functools.partial(_paged_kernel, layer=layer, n_pages=n_pages, n_new=n_new)
    consts = [w['w_ukt'], w['w_uk_c'], w['seg'], w['ones_pe'], w['g_kr'], w['w_uv_p']]
    seq = lambda n1, n2: pl.BlockSpec((None, n1, n2), lambda b, pt: (b, 0, 0))
    cspec = lambda c: pl.BlockSpec(c.shape, lambda b, pt: (0,) * c.ndim)
    return pl.pallas_call(
        kern,
        grid_spec=pltpu.PrefetchScalarGridSpec(
            num_scalar_prefetch=1,
            grid=(nb,),
            in_specs=[seq(n_new, QK_PAD), seq(NEW_PAD, KV_LORA), seq(NEW_PAD, QK_ROPE),
                      pl.BlockSpec(memory_space=pl.ANY), pl.BlockSpec(memory_space=pl.ANY)]
                     + [cspec(c) for c in consts],
            out_specs=seq(n_new, QK_PAD),
            scratch_shapes=[
                pltpu.VMEM((2, past, KV_LORA), F32),
                pltpu.VMEM((2, past, QK_ROPE), F32),
                pltpu.VMEM((past + NEW_PAD, ncol), F32),
                pltpu.VMEM((past + NEW_PAD, KV_LORA), BF16),
                pltpu.SemaphoreType.DMA((2,)),
                pltpu.SemaphoreType.DMA((2,)),
            ],
        ),
        out_shape=jax.ShapeDtypeStruct((nb, n_new, QK_PAD), F32),
        compiler_params=_params("arbitrary"),
        name="mla_paged",
    )(page_table, q, ckv_new, kpe_new, cache_ckv, cache_kpe, *consts)


def _paged_weights(w):
    g_kn = w['g_kn']
    g_pad = jnp.tile(jnp.pad(g_kn[:QK_NOPE], (0, HEAD_PAD - QK_NOPE)), N_HEADS)
    w_uk_c = w['w_uk'].reshape(KV_LORA, N_HEADS, HEAD_PAD)[:, :, :QK_NOPE].reshape(KV_LORA, N_HEADS * QK_NOPE)
    seg = np.zeros((N_HEADS * QK_NOPE, LANES), np.float32)
    for t in range(LANES // N_HEADS):
        for h in range(N_HEADS):
            seg[h * QK_NOPE:(h + 1) * QK_NOPE, t * N_HEADS + h] = 1.0
    return {
        'w_ukt': (w['w_uk'].T * g_pad[:, None]).astype(BF16),
        'w_uk_c': w_uk_c.astype(BF16),
        'seg': jnp.asarray(seg, BF16),
        'ones_pe': jnp.ones((QK_ROPE, LANES), BF16),
        'g_kr': g_kn[QK_NOPE:].reshape(1, QK_ROPE),
        'w_uv_p': w['w_uv'].astype(BF16),
    }


def _out_proj_kernel(x_ref, o_ref, w_ref, y_ref):
    y_ref[...] = x_ref[...] + _dot(o_ref[...].astype(BF16), w_ref[...])


def _out_proj_call(x, o, w_out_pad):
    return pl.pallas_call(
        _out_proj_kernel,
        out_shape=jax.ShapeDtypeStruct(x.shape, F32),
        compiler_params=pltpu.CompilerParams(vmem_limit_bytes=VMEM_LIMIT_BYTES),
        name="mla_out_proj",
    )(x, o, w_out_pad)


def _s5_lambda(ldt, ar, ai):
    dt = jnp.exp(ldt)
    mag = jnp.exp(ar * dt)
    lr = mag * jnp.cos(ai * dt)
    li = mag * jnp.sin(ai * dt)
    den = ar * ar + ai * ai
    cr = ((lr - 1.0) * ar + li * ai) / den
    ci = (li * ar - (lr - 1.0) * ai) / den
    return lr, li, cr, ci


def _s5_disc_kernel(ldt_ref, ar_ref, ai_ref, ldte_ref, are_ref, aie_ref, br_ref, bi_ref,
                    lr_ref, li_ref, bbr_ref, bbi_ref):
    lr, li, _, _ = _s5_lambda(ldt_ref[...], ar_ref[...], ai_ref[...])
    lr_ref[...] = lr
    li_ref[...] = li
    _, _, cr, ci = _s5_lambda(ldte_ref[...], are_ref[...], aie_ref[...])
    br, bi = br_ref[...], bi_ref[...]
    bbr_ref[...] = cr * br - ci * bi
    bbi_ref[...] = cr * bi + ci * br


def _s5_disc_call(log_dt, a_re, a_im, b_re, b_im):
    g, p, c = b_re.shape
    flat = lambda a: jnp.broadcast_to(a[:, :, None], (g, p, c)).reshape(-1, LANES)
    outs = pl.pallas_call(
        _s5_disc_kernel,
        out_shape=[jax.ShapeDtypeStruct((g, p), F32)] * 2 + [jax.ShapeDtypeStruct((g * p * c // LANES, LANES), F32)] * 2,
        name="s5_discretise",
    )(log_dt, a_re, a_im, flat(log_dt), flat(a_re), flat(a_im), b_re.reshape(-1, LANES), b_im.reshape(-1, LANES))
    lr, li, bbr, bbi = outs
    return lr, li, bbr.reshape(g, p, c), bbi.reshape(g, p, c)


S5_GB = 8
S5_CW = S5_GB * SSM_GROUP
S5_SW = S5_GB * SSM_STATE
S5_LC = 1024


def _s5_kernel(x_ref, s0_ref, g_ref, win_ref, wb_ref, lam_ref, wc_ref, dsk_ref, wout_ref,
               o_ref, st_ref, u_ref, xs_ref, y_ref, *, nb, nt, nsub):
    i = pl.program_id(0)
    d = x_ref.shape[-1]
    ns = st_ref.shape[-1] // 2
    nts = nt // nsub
    rs = nts * nb

    @pl.when(i == 0)
    def _():
        st_ref[...] = s0_ref[...]

    x = x_ref[...]
    h = _rms(x, g_ref[...]).astype(BF16)
    u_ref[...] = _dot(h, win_ref[...])

    for sub in range(nsub):
        rows = slice(sub * rs, (sub + 1) * rs)
        for j in range(d // S5_CW):
            xb = _dot(u_ref[rows, j * S5_CW:(j + 1) * S5_CW].astype(BF16), wb_ref[j])
            xs_ref[:, j * S5_SW:(j + 1) * S5_SW] = xb[:, :S5_SW]
            xs_ref[:, ns + j * S5_SW:ns + (j + 1) * S5_SW] = xb[:, S5_SW:]

        def scan_tile(bt, carry):
            b0 = pl.multiple_of(bt * SUBLANES, SUBLANES)
            for lc in range(0, ns, S5_LC):
                re, im = slice(lc, lc + S5_LC), slice(ns + lc, ns + lc + S5_LC)

                def step(t, s):
                    sr, si = s
                    r0 = pl.multiple_of(t * nb + b0, SUBLANES)
                    lr, li = lam_ref[:, re], lam_ref[:, im]
                    nr = lr * sr - li * si + xs_ref[pl.ds(r0, SUBLANES), re]
                    ni = lr * si + li * sr + xs_ref[pl.ds(r0, SUBLANES), im]
                    xs_ref[pl.ds(r0, SUBLANES), re] = nr
                    xs_ref[pl.ds(r0, SUBLANES), im] = ni
                    return nr, ni

                s0 = (st_ref[pl.ds(b0, SUBLANES), re], st_ref[pl.ds(b0, SUBLANES), im])
                sr, si = lax.fori_loop(0, nts, step, s0)
                st_ref[pl.ds(b0, SUBLANES), re] = sr
                st_ref[pl.ds(b0, SUBLANES), im] = si
            return carry

        lax.fori_loop(0, nb // SUBLANES, scan_tile, 0)

        for j in range(d // S5_CW):
            cs = slice(j * S5_CW, (j + 1) * S5_CW)
            s_cat = jnp.concatenate([xs_ref[:, j * S5_SW:(j + 1) * S5_SW],
                                     xs_ref[:, ns + j * S5_SW:ns + (j + 1) * S5_SW]], axis=-1)
            y = _dot(s_cat.astype(BF16), wc_ref[j]) + dsk_ref[:, cs] * u_ref[rows, cs]
            y_ref[rows, cs] = jax.nn.gelu(y).astype(BF16)

    gl = _dot(y_ref[...], wout_ref[...])
    o_ref[...] = x + gl[:, :d] * jax.nn.sigmoid(gl[:, d:])


def _s5_call(x_tm, s0, w, *, nb, nt, nsub):
    rows, d = x_tm.shape
    r = nb * nt
    ns2 = s0.shape[-1]
    assert rows % r == 0 and nt % nsub == 0 and nb % SUBLANES == 0 and d % S5_CW == 0
    kern = functools.partial(_s5_kernel, nb=nb, nt=nt, nsub=nsub)
    consts = [s0, w['g'], w['w_in'], w['wb'], w['lam'], w['wc'], w['dsk'], w['w_out']]
    return pl.pallas_call(
        kern,
        grid=(rows // r,),
        in_specs=[pl.BlockSpec((r, d), lambda i: (i, 0))] + [_const_spec(c.shape) for c in consts],
        out_specs=[pl.BlockSpec((r, d), lambda i: (i, 0)), _const_spec((nb, ns2))],
        out_shape=[jax.ShapeDtypeStruct((rows, d), F32), jax.ShapeDtypeStruct((nb, ns2), F32)],
        scratch_shapes=[
            pltpu.VMEM((r, d), F32),
            pltpu.VMEM((r // nsub, ns2), F32),
            pltpu.VMEM((r, d), BF16),
        ],
        compiler_params=_params("arbitrary"),
        name="s5_mixer",
    )(x_tm, *consts)


def _s5_weights(g, w_in, log_dt, a_re, a_im, b_re, b_im, c_re, c_im, d_skip, w_out):
    d = w_in.shape[0]
    ng, p, c = b_re.shape
    lr, li, bbr, bbi = _s5_disc_call(log_dt, a_re, a_im, b_re, b_im)
    eye = jnp.eye(S5_GB, dtype=F32)
    nblk = ng // S5_GB

    def b_tiles(bb):
        t = bb.reshape(nblk, S5_GB, p, c).transpose(0, 1, 3, 2)
        return jnp.einsum('ab,jacp->jacbp', eye, t).reshape(nblk, S5_CW, S5_SW)

    def c_tiles(cc):
        t = cc.reshape(nblk, S5_GB, c, p)
        return jnp.einsum('ba,jacp->jbpac', eye, t).reshape(nblk, S5_SW, S5_CW)

    lam = jnp.concatenate([lr.reshape(1, -1), li.reshape(1, -1)], axis=-1)
    return {
        'g': g.reshape(1, d),
        'w_in': w_in.astype(BF16),
        'wb': jnp.concatenate([b_tiles(bbr), b_tiles(bbi)], axis=-1).astype(BF16),
        'lam': jnp.broadcast_to(lam, (SUBLANES, lam.shape[-1])),
        'wc': jnp.concatenate([c_tiles(c_re), -c_tiles(c_im)], axis=1).astype(BF16),
        'dsk': d_skip.reshape(1, d),
        'w_out': w_out.astype(BF16),
    }


PROMPT_TM = 512
FLASH_TQ = 512
FLASH_TK = 512
FFN_TM = 256
S5_ROWS = 512
S5_NSUB = 2


def _mla_prompt_layer(x, w):
    b, s, d = x.shape
    tabs = _rope_tables(jnp.arange(s))
    q, k, v, ckv, kpe = _mla_proj_call(x, tabs, w, tm=min(PROMPT_TM, s))
    y = _flash_call(q, k, v, x, w['w_out'], tq=min(FLASH_TQ, s), tk=min(FLASH_TK, s))
    return y, ckv, kpe


def _mla_sample_layer(x, page_table, cache_ckv, cache_kpe, w, *, layer):
    b, t, d = x.shape
    past = page_table.shape[1] * PAGE_SIZE
    tabs = [jnp.tile(tb, (b, 1)) for tb in _rope_tables(past + jnp.arange(t))]
    q, _, _, ckv, kpe = _mla_proj_call(x.reshape(1, b * t, d), tabs, w, tm=b * t)
    ckv, kpe = ckv.reshape(b, t, KV_LORA), kpe.reshape(b, t, QK_ROPE)
    pad = lambda a: jnp.pad(a, ((0, 0), (0, NEW_PAD - t), (0, 0)))
    o = _paged_call(page_table, q.reshape(b, t, QK_PAD).astype(F32), pad(ckv), pad(kpe), cache_ckv, cache_kpe,
                    _paged_weights(w), layer=layer)
    y = _out_proj_call(x.reshape(b * t, d), o.reshape(b * t, QK_PAD), w['w_out'])
    return y.reshape(b, t, d), ckv, kpe


def _s5_layer(x, s0_re, s0_im, w):
    b, t, d = x.shape
    nt = S5_ROWS // b
    x_tm = jnp.swapaxes(x, 0, 1).reshape(t * b, d)
    s0 = jnp.concatenate([s0_re.reshape(b, -1), s0_im.reshape(b, -1)], axis=-1)
    y_tm, st = _s5_call(x_tm, s0, w, nb=b, nt=min(nt, t), nsub=S5_NSUB)
    ns = st.shape[-1] // 2
    y = jnp.swapaxes(y_tm.reshape(t, b, d), 0, 1)
    return y, st[:, :ns].reshape(s0_re.shape), st[:, ns:].reshape(s0_im.shape)


def _ffn_prompt_layer(x, w):
    b, s, d = x.shape
    zero = jnp.zeros((b, CONV_W - 1, w[1].shape[-1]), F32)
    return _ffn_call(x, zero, *w, nb=1, tm=min(FFN_TM, s))


def _ffn_sample_layer(x, prev, w):
    b, t, d = x.shape
    x_tm = jnp.swapaxes(x, 0, 1).reshape(1, t * b, d)
    prev_tm = jnp.swapaxes(prev, 0, 1).reshape(1, (CONV_W - 1) * b, prev.shape[-1])
    y, st = _ffn_call(x_tm, prev_tm, *w, nb=b, tm=t * b)
    return jnp.swapaxes(y.reshape(t, b, d), 0, 1), jnp.swapaxes(st.reshape(CONV_W - 1, b, -1), 0, 1)


def kernel(x_prompt, x_sample, cache_ckv, cache_kpe, state_ssm_re, state_ssm_im, state_conv, page_table,
           attn_norm, attn_w_in, attn_q_lat_norm, attn_w_q_up, attn_kv_lat_norm, attn_w_kv_up,
           attn_q_norm, attn_k_norm, attn_w_out,
           ssm_norm, ssm_w_in, ssm_log_dt, ssm_a_re, ssm_a_im, ssm_b_re, ssm_b_im,
           ssm_c_re, ssm_c_im, ssm_d, ssm_w_out,
           ffn_norm, ffn_w_up, ffn_conv_w, ffn_conv_b, ffn_w_down):
    xp, xs = x_prompt, x_sample
    depth = ffn_w_up.shape[0]
    bp = xp.shape[0]
    ssm_zero = jnp.zeros((bp,) + state_ssm_re.shape[2:], F32)
    ckv_p, kpe_p, ckv_s, kpe_s = [], [], [], []
    sre_p, sim_p, sre_s, sim_s = [], [], [], []
    conv_p, conv_s = [], []
    for i in range(depth):
        j = i // 2
        if i % 2 == 0:
            w = _mla_weights(attn_norm[j], attn_w_in[j], attn_q_lat_norm[j], attn_w_q_up[j], attn_kv_lat_norm[j],
                             attn_w_kv_up[j], attn_q_norm[j], attn_k_norm[j], attn_w_out[j])
            xp, c_p, r_p = _mla_prompt_layer(xp, w)
            xs, c_s, r_s = _mla_sample_layer(xs, page_table, cache_ckv, cache_kpe, w, layer=j)
            ckv_p.append(c_p)
            kpe_p.append(r_p)
            ckv_s.append(c_s)
            kpe_s.append(r_s)
        else:
            w = _s5_weights(ssm_norm[j], ssm_w_in[j], ssm_log_dt[j], ssm_a_re[j], ssm_a_im[j], ssm_b_re[j],
                            ssm_b_im[j], ssm_c_re[j], ssm_c_im[j], ssm_d[j], ssm_w_out[j])
            xp, r_p, i_p = _s5_layer(xp, ssm_zero, ssm_zero, w)
            xs, r_s, i_s = _s5_layer(xs, state_ssm_re[j], state_ssm_im[j], w)
            sre_p.append(r_p)
            sim_p.append(i_p)
            sre_s.append(r_s)
            sim_s.append(i_s)
        fw = (ffn_norm[i], ffn_w_up[i].astype(BF16), ffn_conv_w[i], ffn_conv_b[i], ffn_w_down[i].astype(BF16))
        xp, cv_p = _ffn_prompt_layer(xp, fw)
        xs, cv_s = _ffn_sample_layer(xs, state_conv[i], fw)
        conv_p.append(cv_p)
        conv_s.append(cv_s)
    return (xp, xs,
            jnp.stack(ckv_p), jnp.stack(kpe_p), jnp.stack(sre_p), jnp.stack(sim_p), jnp.stack(conv_p),
            jnp.stack(ckv_s), jnp.stack(kpe_s), jnp.stack(sre_s), jnp.stack(sim_s), jnp.stack(conv_s))
```

```python
import functools
import math

import jax
import jax.numpy as jnp
import numpy as np
from jax import lax
from jax.experimental import pallas as pl
from jax.experimental.pallas import tpu as pltpu

LANES = 128
SUBLANES = 8
VMEM_LIMIT_BYTES = 56 * 1024 * 1024

N_HEADS = 8
QK_NOPE = 64
QK_ROPE = 32
QK_HEAD = QK_NOPE + QK_ROPE
V_HEAD = 64
HEAD_PAD = LANES
QK_PAD = N_HEADS * HEAD_PAD
Q_LORA = 384
KV_LORA = 256
PAGE_SIZE = 128
ROPE_THETA = 10000.0
SSM_GROUP = 16
SSM_STATE = 64
CONV_W = 3
NORM_EPS = 1e-6

BF16 = jnp.bfloat16
F32 = jnp.float32


def _params(*sem):
    return pltpu.CompilerParams(dimension_semantics=sem, vmem_limit_bytes=VMEM_LIMIT_BYTES)


def _dot(a, b):
    return jnp.dot(a, b, preferred_element_type=F32)


def _dot_nt(a, b):
    return lax.dot_general(a, b, (((1,), (1,)), ((), ())), preferred_element_type=F32)


def _dot_tn(a, b):
    return lax.dot_general(a, b, (((0,), (0,)), ((), ())), preferred_element_type=F32)


def _rms(x, g):
    return x * lax.rsqrt(jnp.mean(x * x, axis=-1, keepdims=True) + NORM_EPS) * g


def _const_spec(shape):
    n = len(shape)
    return pl.BlockSpec(shape, lambda *_: (0,) * n)


FFN_CHUNK = 256


def _ffn_kernel(x_ref, prev_ref, g_ref, wup_ref, cw_ref, cb_ref, wdn_ref, o_ref, st_ref, h_ref, uc_ref, a_ref,
                *, nb, tm, off, d_ff, nc):
    hist = (CONV_W - 1) * nb

    @pl.when(pl.program_id(1) == 0)
    def _():
        st_ref[...] = prev_ref[...]

    x = x_ref[...]
    h_ref[...] = _rms(x, g_ref[...]).astype(BF16)

    for ci, c0 in enumerate(range(0, d_ff, nc)):
        def conv(half, lo):
            buf = uc_ref.at[2 * (ci % 2) + half]
            u = _dot(h_ref[...], wup_ref[:, lo:lo + nc])
            buf[off - hist:off, :] = st_ref[:, lo:lo + nc]
            buf[off:off + tm, :] = u
            st_ref[:, lo:lo + nc] = u[tm - hist:, :]
            c = cb_ref[:, lo:lo + nc]
            for j in range(CONV_W - 1):
                r0 = off - (CONV_W - 1 - j) * nb
                c = c + cw_ref[j:j + 1, lo:lo + nc] * buf[r0:r0 + tm, :]
            return c + cw_ref[CONV_W - 1:CONV_W, lo:lo + nc] * u
        val = conv(0, c0)
        gate = conv(1, d_ff + c0)
        a_ref[:, c0:c0 + nc] = (jax.nn.silu(gate) * val).astype(BF16)

    o_ref[...] = x + _dot(a_ref[...], wdn_ref[...])


def _ffn_call(x, prev, g, w_up, conv_w, conv_b, w_down, *, nb, tm):
    nseq, rows, d = x.shape
    d_ff = w_down.shape[0]
    hist = (CONV_W - 1) * nb
    off = -(-hist // SUBLANES) * SUBLANES
    nc = FFN_CHUNK
    assert rows % tm == 0 and tm % nb == 0 and d_ff % nc == 0 and tm >= hist
    kern = functools.partial(_ffn_kernel, nb=nb, tm=tm, off=off, d_ff=d_ff, nc=nc)
    return pl.pallas_call(
        kern,
        grid=(nseq, rows // tm),
        in_specs=[
            pl.BlockSpec((None, tm, d), lambda s, i: (s, i, 0)),
            pl.BlockSpec((None, hist, 2 * d_ff), lambda s, i: (s, 0, 0)),
            _const_spec((1, d)),
            _const_spec((d, 2 * d_ff)),
            _const_spec((CONV_W, 2 * d_ff)),
            _const_spec((1, 2 * d_ff)),
            _const_spec((d_ff, d)),
        ],
        out_specs=[
            pl.BlockSpec((None, tm, d), lambda s, i: (s, i, 0)),
            pl.BlockSpec((None, hist, 2 * d_ff), lambda s, i: (s, 0, 0)),
        ],
        out_shape=[
            jax.ShapeDtypeStruct((nseq, rows, d), F32),
            jax.ShapeDtypeStruct((nseq, hist, 2 * d_ff), F32),
        ],
        scratch_shapes=[
            pltpu.VMEM((tm, d), BF16),
            pltpu.VMEM((4, off + tm, nc), F32),
            pltpu.VMEM((tm, d_ff), BF16),
        ],
        compiler_params=_params("arbitrary", "arbitrary"),
        name="conv_ffn",
    )(x, prev, g.reshape(1, d), w_up, conv_w, conv_b.reshape(1, 2 * d_ff), w_down)


def _rope128(x, c, sa, sb):
    n = x.shape[-1]
    return x * c + pltpu.roll(x, n - QK_ROPE // 2, 1) * sa + pltpu.roll(x, QK_ROPE // 2, 1) * sb


def _head_ms(x, ones2):
    sq = (x * x).astype(BF16)
    parts = [_dot(sq[:, c0:c0 + 2 * HEAD_PAD], ones2) for c0 in range(0, x.shape[-1], 2 * HEAD_PAD)]
    return jnp.concatenate(parts, axis=-1) * (1.0 / QK_HEAD)


def _mla_proj_kernel(x_ref, tc_ref, tsa_ref, tsb_ref, gin_ref, win_ref, gql_ref, wq_ref, gkvl_ref, wkv_ref,
                     gq_ref, gk_ref, ones2_ref,
                     q_ref, k_ref, v_ref, ckv_ref, kpe_ref):
    x = x_ref[...]
    h = _rms(x, gin_ref[...]).astype(BF16)
    proj = _dot(h, win_ref[...])
    cq = proj[:, :Q_LORA]
    ckv = proj[:, Q_LORA:Q_LORA + KV_LORA]
    kp = proj[:, Q_LORA + KV_LORA:]
    tc, tsa, tsb = tc_ref[...], tsa_ref[...], tsb_ref[...]
    ones2 = ones2_ref[...]

    q = _dot(_rms(cq, gql_ref[...]).astype(BF16), wq_ref[...])
    rep = lambda t: jnp.concatenate([t] * N_HEADS, axis=-1)
    q = _rope128(q, rep(tc), rep(tsa), rep(tsb))
    q = q * lax.rsqrt(_head_ms(q, ones2) + NORM_EPS) * gq_ref[...]
    q_ref[...] = q.astype(q_ref.dtype)

    ckv_n = _rms(ckv, gkvl_ref[...])
    ckv_ref[...] = ckv_n
    kp = _rope128(kp, tc, tsa, tsb)
    kpe_ref[...] = kp[:, QK_NOPE:QK_HEAD]

    kv = _dot(ckv_n.astype(BF16), wkv_ref[...])
    k = kv[:, :QK_PAD] + rep(kp)
    k = k * lax.rsqrt(_head_ms(k, ones2) + NORM_EPS) * gk_ref[...]
    k_ref[...] = k.astype(k_ref.dtype)
    v_ref[...] = kv[:, QK_PAD:].astype(v_ref.dtype)


def _mla_proj_call(x, tabs, w, *, tm):
    nseq, rows, d = x.shape
    assert rows % tm == 0
    row_spec = lambda n: pl.BlockSpec((None, tm, n), lambda s, i: (s, i, 0))
    tab_spec = pl.BlockSpec((tm, LANES), lambda s, i: (i, 0))
    consts = [w['g_in'], w['w_in'], w['g_ql'], w['w_q'], w['g_kvl'], w['w_kv'], w['g_q'], w['g_k'], w['ones2']]
    return pl.pallas_call(
        _mla_proj_kernel,
        grid=(nseq, rows // tm),
        in_specs=[row_spec(d), tab_spec, tab_spec, tab_spec] + [_const_spec(c.shape) for c in consts],
        out_specs=[row_spec(QK_PAD), row_spec(QK_PAD), row_spec(QK_PAD), row_spec(KV_LORA), row_spec(QK_ROPE)],
        out_shape=[
            jax.ShapeDtypeStruct((nseq, rows, QK_PAD), BF16),
            jax.ShapeDtypeStruct((nseq, rows, QK_PAD), BF16),
            jax.ShapeDtypeStruct((nseq, rows, QK_PAD), BF16),
            jax.ShapeDtypeStruct((nseq, rows, KV_LORA), F32),
            jax.ShapeDtypeStruct((nseq, rows, QK_ROPE), F32),
        ],
        compiler_params=_params("arbitrary", "arbitrary"),
        name="mla_proj",
    )(x, *tabs, *consts)


def _head_pad_cols(w, per_head, lo, hi):
    k = w.shape[0]
    w = w.reshape(k, N_HEADS, per_head)[:, :, lo:hi]
    w = jnp.pad(w, ((0, 0), (0, 0), (0, HEAD_PAD - (hi - lo))))
    return w.reshape(k, QK_PAD)


def _head_pad_vec(g):
    return jnp.tile(jnp.pad(g, (0, HEAD_PAD - g.shape[0])), N_HEADS).reshape(1, QK_PAD)


def _rope_tables(pos):
    half = QK_ROPE // 2
    inv = ROPE_THETA ** (-jnp.arange(half, dtype=F32) * (2.0 / QK_ROPE))
    ang = pos.astype(F32)[:, None] * inv[None, :]
    cos, sin = jnp.cos(ang), jnp.sin(ang)
    t = pos.shape[0]
    z = lambda n: jnp.zeros((t, n), F32)
    c = jnp.concatenate([jnp.ones((t, QK_NOPE), F32), cos, cos, z(HEAD_PAD - QK_HEAD)], axis=-1)
    sa = jnp.concatenate([z(QK_NOPE), -sin, z(half), z(HEAD_PAD - QK_HEAD)], axis=-1)
    sb = jnp.concatenate([z(QK_NOPE), z(half), sin, z(HEAD_PAD - QK_HEAD)], axis=-1)
    return c, sa, sb


def _mla_weights(g_in, w_in, g_ql, w_q_up, g_kvl, w_kv_up, g_qn, g_kn, w_out):
    d = w_in.shape[0]
    w_kpe = jnp.pad(w_in[:, Q_LORA + KV_LORA:], ((0, 0), (QK_NOPE, HEAD_PAD - QK_HEAD)))
    w_in_ext = jnp.concatenate([w_in[:, :Q_LORA + KV_LORA], w_kpe], axis=1)
    w_uk = _head_pad_cols(w_kv_up, QK_NOPE + V_HEAD, 0, QK_NOPE)
    w_uv = _head_pad_cols(w_kv_up, QK_NOPE + V_HEAD, QK_NOPE, QK_NOPE + V_HEAD)
    blk = np.kron(np.eye(2, dtype=np.float32), np.ones((HEAD_PAD, HEAD_PAD), np.float32))
    w_out_pad = jnp.pad(w_out.reshape(N_HEADS, V_HEAD, d), ((0, 0), (0, HEAD_PAD - V_HEAD), (0, 0)))
    return {
        'g_in': g_in.reshape(1, d),
        'w_in': w_in_ext.astype(BF16),
        'g_ql': g_ql.reshape(1, Q_LORA),
        'w_q': _head_pad_cols(w_q_up, QK_HEAD, 0, QK_HEAD).astype(BF16),
        'g_kvl': g_kvl.reshape(1, KV_LORA),
        'w_kv': jnp.concatenate([w_uk, w_uv], axis=1).astype(BF16),
        'g_q': _head_pad_vec(g_qn) * (QK_HEAD ** -0.5 * math.log2(math.e)),
        'g_k': _head_pad_vec(g_kn),
        'ones2': jnp.asarray(blk, BF16),
        'w_out': w_out_pad.reshape(QK_PAD, d).astype(BF16),
        'w_uk': w_uk, 'w_uv': w_uv, 'g_kn': g_kn,
    }


def _flash_kernel(q_ref, k_ref, v_ref, x_ref, wo_ref, o_ref, m_ref, l_ref, acc_ref, *, tq, qs):
    tk = tq
    i = pl.program_id(1)
    j = pl.program_id(2)

    @pl.when(j == 0)
    def _():
        m_ref[...] = jnp.full(m_ref.shape, -jnp.inf, F32)
        l_ref[...] = jnp.zeros(l_ref.shape, F32)
        acc_ref[...] = jnp.zeros(acc_ref.shape, F32)

    def step(diagonal):
        keep = {}
        if diagonal:
            for r0 in range(0, tq, qs):
                row = r0 + lax.broadcasted_iota(jnp.int32, (qs, r0 + qs), 0)
                keep[r0] = lax.broadcasted_iota(jnp.int32, (qs, r0 + qs), 1) <= row
        for h in range(N_HEADS):
            hs = slice(h * HEAD_PAD, (h + 1) * HEAD_PAD)
            for r0 in range(0, tq, qs):
                rs = slice(r0, r0 + qs)
                nk = r0 + qs if diagonal else tk
                s = _dot_nt(q_ref[rs, hs], k_ref[:nk, hs])
                if diagonal:
                    s = jnp.where(keep[r0], s, jnp.finfo(F32).min)
                m_old = m_ref[h, rs]
                m_new = jnp.maximum(m_old, jnp.max(s, axis=-1, keepdims=True))
                alpha = jnp.exp2(m_old - m_new)
                p = jnp.exp2(s - m_new[:, :1])
                l_ref[h, rs] = alpha * l_ref[h, rs] + jnp.sum(p, axis=-1, keepdims=True)
                acc_ref[h, rs] = alpha * acc_ref[h, rs] + _dot(p.astype(BF16), v_ref[:nk, hs])
                m_ref[h, rs] = m_new

    @pl.when(j < i)
    def _():
        step(False)

    @pl.when(j == i)
    def _():
        step(True)
        o = jnp.concatenate([acc_ref[h] / l_ref[h] for h in range(N_HEADS)], axis=-1)
        o_ref[...] = x_ref[...] + _dot(o.astype(BF16), wo_ref[...])


def _flash_call(q, k, v, x, w_out_pad, *, tq, qs):
    nseq, s, d = x.shape
    tk = tq
    assert s % tq == 0 and tq % qs == 0
    kern = functools.partial(_flash_kernel, tq=tq, qs=qs)
    kv_map = lambda b, i, j: (b, jnp.minimum(j, i), 0)
    return pl.pallas_call(
        kern,
        grid=(nseq, s // tq, s // tk),
        in_specs=[
            pl.BlockSpec((None, tq, QK_PAD), lambda b, i, j: (b, i, 0)),
            pl.BlockSpec((None, tk, QK_PAD), kv_map),
            pl.BlockSpec((None, tk, QK_PAD), kv_map),
            pl.BlockSpec((None, tq, d), lambda b, i, j: (b, i, 0)),
            _const_spec((QK_PAD, d)),
        ],
        out_specs=pl.BlockSpec((None, tq, d), lambda b, i, j: (b, i, 0)),
        out_shape=jax.ShapeDtypeStruct((nseq, s, d), F32),
        scratch_shapes=[
            pltpu.VMEM((N_HEADS, tq, HEAD_PAD), F32),
            pltpu.VMEM((N_HEADS, tq, HEAD_PAD), F32),
            pltpu.VMEM((N_HEADS, tq, HEAD_PAD), F32),
        ],
        compiler_params=_params("arbitrary", "arbitrary", "arbitrary"),
        name="mla_flash",
    )(q, k, v, x, w_out_pad)


NEW_PAD = 16
KEY_TILE = 1024


def _paged_kernel(pt_ref, q_ref, cnew_ref, rnew_ref, ckv_hbm, kpe_hbm, wukt_ref, wuk_ref, seg_ref,
                  gkr_ref, wuv_ref, o_ref, cbuf, rbuf, s_ref, cbf_ref, csem, rsem, *, layer, n_pages, n_new):
    b = pl.program_id(0)
    nb = pl.num_programs(0)
    slot = b % 2
    past = n_pages * PAGE_SIZE
    n_tiles = past // KEY_TILE
    ppt = KEY_TILE // PAGE_SIZE
    ncol = s_ref.shape[-1]

    def start_tile(bb, sl, i):
        for k in range(ppt):
            p = i * ppt + k
            pg = pt_ref[bb, p]
            rows = pl.ds(pl.multiple_of(p * PAGE_SIZE, PAGE_SIZE), PAGE_SIZE)
            pltpu.make_async_copy(ckv_hbm.at[layer, pg], cbuf.at[sl, rows, :], csem.at[sl]).start()
            pltpu.make_async_copy(kpe_hbm.at[layer, pg], rbuf.at[sl, i, :, pl.ds(k * PAGE_SIZE, PAGE_SIZE)],
                                  rsem.at[sl]).start()

    @pl.when(b == 0)
    def _():
        def body(i, c):
            start_tile(0, 0, i)
            return c
        lax.fori_loop(0, n_tiles, body, 0)

    pltpu.make_async_copy(cbuf.at[slot], cbuf.at[slot], csem.at[slot]).wait()
    pltpu.make_async_copy(rbuf.at[slot], rbuf.at[slot], rsem.at[slot]).wait()

    nq = n_new * N_HEADS
    q = q_ref[...]
    qb = jnp.concatenate([jnp.broadcast_to(q[t:t + 1], (N_HEADS, QK_PAD)) for t in range(n_new)], axis=0)
    row_head = lax.broadcasted_iota(jnp.int32, (nq, QK_PAD), 0) % N_HEADS
    lane_head = lax.broadcasted_iota(jnp.int32, (nq, QK_PAD), 1) // HEAD_PAD
    own_head = row_head == lane_head
    qb = jnp.where(own_head, qb, 0.0)
    qb = jnp.concatenate([qb, jnp.zeros((ncol - nq, QK_PAD), F32)], axis=0)
    qlat = _dot(qb.astype(BF16), wukt_ref[...]).astype(BF16)
    qsum = qb[:, :HEAD_PAD]
    for h in range(1, N_HEADS):
        qsum = qsum + qb[:, h * HEAD_PAD:(h + 1) * HEAD_PAD]
    q_rope_t = (qsum * gkr_ref[...]).T[QK_NOPE:QK_HEAD]
    zeros = jnp.zeros((QK_ROPE, ncol), F32)
    wpe = jnp.concatenate([jnp.concatenate([zeros, q_rope_t], axis=1),
                           jnp.concatenate([zeros + 1.0, zeros], axis=1)], axis=0).astype(BF16)

    def scores(ckv_t, kpe_tt):
        cb = ckv_t.astype(BF16)
        kn = _dot(cb, wuk_ref[...])
        k2 = kn * kn
        k2s = k2[:, :LANES]
        for c0 in range(LANES, k2.shape[-1], LANES):
            k2s = k2s + k2[:, c0:c0 + LANES]
        pe = _dot_tn(jnp.concatenate([kpe_tt, kpe_tt * kpe_tt], axis=0).astype(BF16), wpe)
        ssq = _dot(k2s.astype(BF16), seg_ref[...]) + pe[:, :ncol]
        s = _dot_nt(cb, qlat) + pe[:, ncol:]
        return s * lax.rsqrt(ssq * (1.0 / QK_HEAD) + NORM_EPS), cb

    def tile_rows(i):
        return pl.ds(pl.multiple_of(i * KEY_TILE, KEY_TILE), KEY_TILE)

    nxt = jnp.minimum(b + 1, nb - 1)

    def pass1(i, m):
        start_tile(nxt, 1 - slot, i)
        rows = tile_rows(i)
        s, cb = scores(cbuf[slot, rows, :], rbuf[slot, i])
        s_ref[rows, :] = s
        cbf_ref[rows, :] = cb
        return jnp.maximum(m, jnp.max(s, axis=0, keepdims=True))

    m = lax.fori_loop(0, n_tiles, pass1, jnp.full((1, ncol), -jnp.inf, F32), unroll=2)

    @pl.when(b == nb - 1)
    def _():
        pltpu.make_async_copy(cbuf.at[1 - slot], cbuf.at[1 - slot], csem.at[1 - slot]).wait()
        pltpu.make_async_copy(rbuf.at[1 - slot], rbuf.at[1 - slot], rsem.at[1 - slot]).wait()

    s_new, cb_new = scores(cnew_ref[...], rnew_ref[...])
    key = lax.broadcasted_iota(jnp.int32, (NEW_PAD, ncol), 0)
    qt = lax.broadcasted_iota(jnp.int32, (NEW_PAD, ncol), 1) // N_HEADS
    s_new = jnp.where((key < n_new) & (key <= qt), s_new, jnp.finfo(F32).min)
    m = jnp.maximum(m, jnp.max(s_new, axis=0, keepdims=True))
    new_rows = slice(past, past + NEW_PAD)
    cbf_ref[new_rows, :] = cb_new

    def pass2(i, l):
        rows = tile_rows(i)
        p = jnp.exp2(s_ref[rows, :] - m)
        s_ref[rows, :] = p
        return l + jnp.sum(p, axis=0, keepdims=True)

    l = lax.fori_loop(0, n_tiles, pass2, jnp.zeros((1, ncol), F32), unroll=2)
    p_new = jnp.exp2(s_new - m)
    l = l + jnp.sum(p_new, axis=0, keepdims=True)
    inv = 1.0 / l

    def pass3(i, acc):
        rows = tile_rows(i)
        return acc + _dot_tn((s_ref[rows, :] * inv).astype(BF16), cbf_ref[rows, :])

    o_lat = _dot_tn((p_new * inv).astype(BF16), cbf_ref[new_rows, :])
    o_lat = lax.fori_loop(0, n_tiles, pass3, o_lat, unroll=2)

    o = _dot(o_lat[:nq].astype(BF16), wuv_ref[...])
    o = jnp.where(own_head, o, 0.0)
    o_ref[...] = jnp.sum(o.reshape(n_new, N_HEADS, QK_PAD), axis=1)


def _paged_call(page_table, q, ckv_new, kpe_new, cache_ckv, cache_kpe, w, *, layer):
    nb, n_new, _ = q.shape
    n_pages = page_table.shape[1]
    past = n_pages * PAGE_SIZE
    assert past % KEY_TILE == 0 and n_new <= NEW_PAD
    ncol = LANES
    assert n_new * N_HEADS <= ncol
    kern = functools.partial(_paged_kernel, layer=layer, n_pages=n_pages, n_new=n_new)
    consts = [w['w_ukt'], w['w_uk_r'], w['seg'], w['g_kr'], w['w_uv_p']]
    seq = lambda n1, n2: pl.BlockSpec((None, n1, n2), lambda b, pt: (b, 0, 0))
    cspec = lambda c: pl.BlockSpec(c.shape, lambda b, pt: (0,) * c.ndim)
    return pl.pallas_call(
        kern,
        grid_spec=pltpu.PrefetchScalarGridSpec(
            num_scalar_prefetch=1,
            grid=(nb,),
            in_specs=[seq(n_new, QK_PAD), seq(NEW_PAD, KV_LORA), seq(QK_ROPE, NEW_PAD),
                      pl.BlockSpec(memory_space=pl.ANY), pl.BlockSpec(memory_space=pl.ANY)]
                     + [cspec(c) for c in consts],
            out_specs=seq(n_new, QK_PAD),
            scratch_shapes=[
                pltpu.VMEM((2, past, KV_LORA), F32),
                pltpu.VMEM((2, past // KEY_TILE, QK_ROPE, KEY_TILE), F32),
                pltpu.VMEM((past + NEW_PAD, ncol), F32),
                pltpu.VMEM((past + NEW_PAD, KV_LORA), BF16),
                pltpu.SemaphoreType.DMA((2,)),
                pltpu.SemaphoreType.DMA((2,)),
            ],
        ),
        out_shape=jax.ShapeDtypeStruct((nb, n_new, QK_PAD), F32),
        compiler_params=_params("arbitrary"),
        name="mla_paged",
    )(page_table, q, ckv_new, kpe_new, cache_ckv, cache_kpe, *consts)


def _paged_weights(w):
    g_kn = w['g_kn']
    g_pad = jnp.tile(jnp.pad(g_kn[:QK_NOPE], (0, HEAD_PAD - QK_NOPE)), N_HEADS)
    sub = LANES // N_HEADS
    w_uk_r = w['w_uk'].reshape(KV_LORA, N_HEADS, HEAD_PAD)[:, :, :QK_NOPE]
    w_uk_r = w_uk_r.reshape(KV_LORA, N_HEADS, QK_NOPE // sub, sub).transpose(0, 2, 1, 3)
    seg = np.zeros((LANES, LANES), np.float32)
    for t in range(LANES // N_HEADS):
        for h in range(N_HEADS):
            seg[h * sub:(h + 1) * sub, t * N_HEADS + h] = 1.0
    return {
        'w_ukt': (w['w_uk'].T * g_pad[:, None]).astype(BF16),
        'w_uk_r': w_uk_r.reshape(KV_LORA, N_HEADS * QK_NOPE).astype(BF16),
        'seg': jnp.asarray(seg, BF16),
        'g_kr': jnp.pad(g_kn[QK_NOPE:], (QK_NOPE, HEAD_PAD - QK_HEAD)).reshape(1, HEAD_PAD),
        'w_uv_p': w['w_uv'].astype(BF16),
    }


def _out_proj_kernel(x_ref, o_ref, w_ref, y_ref):
    y_ref[...] = x_ref[...] + _dot(o_ref[...].astype(BF16), w_ref[...])


def _out_proj_call(x, o, w_out_pad):
    return pl.pallas_call(
        _out_proj_kernel,
        out_shape=jax.ShapeDtypeStruct(x.shape, F32),
        compiler_params=pltpu.CompilerParams(vmem_limit_bytes=VMEM_LIMIT_BYTES),
        name="mla_out_proj",
    )(x, o, w_out_pad)


def _s5_lambda(ldt, ar, ai):
    dt = jnp.exp(ldt)
    mag = jnp.exp(ar * dt)
    lr = mag * jnp.cos(ai * dt)
    li = mag * jnp.sin(ai * dt)
    den = ar * ar + ai * ai
    cr = ((lr - 1.0) * ar + li * ai) / den
    ci = (li * ar - (lr - 1.0) * ai) / den
    return lr, li, cr, ci


def _s5_disc_kernel(ldt_ref, ar_ref, ai_ref, ldte_ref, are_ref, aie_ref, br_ref, bi_ref,
                    lr_ref, li_ref, bbr_ref, bbi_ref):
    lr, li, _, _ = _s5_lambda(ldt_ref[...], ar_ref[...], ai_ref[...])
    lr_ref[...] = lr
    li_ref[...] = li
    _, _, cr, ci = _s5_lambda(ldte_ref[...], are_ref[...], aie_ref[...])
    br, bi = br_ref[...], bi_ref[...]
    bbr_ref[...] = cr * br - ci * bi
    bbi_ref[...] = cr * bi + ci * br


def _s5_disc_call(log_dt, a_re, a_im, b_re, b_im):
    g, p, c = b_re.shape
    flat = lambda a: jnp.broadcast_to(a[:, :, None], (g, p, c)).reshape(-1, LANES)
    outs = pl.pallas_call(
        _s5_disc_kernel,
        out_shape=[jax.ShapeDtypeStruct((g, p), F32)] * 2 + [jax.ShapeDtypeStruct((g * p * c // LANES, LANES), F32)] * 2,
        name="s5_discretise",
    )(log_dt, a_re, a_im, flat(log_dt), flat(a_re), flat(a_im), b_re.reshape(-1, LANES), b_im.reshape(-1, LANES))
    lr, li, bbr, bbi = outs
    return lr, li, bbr.reshape(g, p, c), bbi.reshape(g, p, c)


S5_GB = 8
S5_CW = S5_GB * SSM_GROUP
S5_SW = S5_GB * SSM_STATE
S5_LC = 1024


def _s5_kernel(x_ref, s0_ref, perm_ref, permt_ref, g_ref, win_ref, wb_ref, lam_ref, wc_ref, dsk_ref, wout_ref,
               o_ref, st_ref, u_ref, xs_ref, y_ref, *, nb, nt, nsub):
    i = pl.program_id(0)
    d = x_ref.shape[-1]
    ns = st_ref.shape[-1] // 2
    nts = nt // nsub
    rs = nts * nb

    @pl.when(i == 0)
    def _():
        st_ref[...] = s0_ref[...]

    x = x_ref[...].reshape(nb * nt, d)
    h = _rms(x, g_ref[...]).astype(BF16)
    h = _dot(perm_ref[...], h).astype(BF16)
    u_ref[...] = _dot(h, win_ref[...])

    for sub in range(nsub):
        rows = slice(sub * rs, (sub + 1) * rs)
        for j in range(d // S5_CW):
            xb = _dot(u_ref[rows, j * S5_CW:(j + 1) * S5_CW].astype(BF16), wb_ref[j])
            xs_ref[:, j * S5_SW:(j + 1) * S5_SW] = xb[:, :S5_SW]
            xs_ref[:, ns + j * S5_SW:ns + (j + 1) * S5_SW] = xb[:, S5_SW:]

        def scan_tile(bt, carry):
            b0 = pl.multiple_of(bt * SUBLANES, SUBLANES)
            for lc in range(0, ns, S5_LC):
                re, im = slice(lc, lc + S5_LC), slice(ns + lc, ns + lc + S5_LC)

                def step(t, s):
                    sr, si = s
                    r0 = pl.multiple_of(t * nb + b0, SUBLANES)
                    lr, li = lam_ref[:, re], lam_ref[:, im]
                    nr = lr * sr - li * si + xs_ref[pl.ds(r0, SUBLANES), re]
                    ni = lr * si + li * sr + xs_ref[pl.ds(r0, SUBLANES), im]
                    xs_ref[pl.ds(r0, SUBLANES), re] = nr
                    xs_ref[pl.ds(r0, SUBLANES), im] = ni
                    return nr, ni

                s0 = (st_ref[pl.ds(b0, SUBLANES), re], st_ref[pl.ds(b0, SUBLANES), im])
                sr, si = lax.fori_loop(0, nts, step, s0)
                st_ref[pl.ds(b0, SUBLANES), re] = sr
                st_ref[pl.ds(b0, SUBLANES), im] = si
            return carry

        lax.fori_loop(0, nb // SUBLANES, scan_tile, 0)

        for j in range(d // S5_CW):
            cs = slice(j * S5_CW, (j + 1) * S5_CW)
            s_cat = jnp.concatenate([xs_ref[:, j * S5_SW:(j + 1) * S5_SW],
                                     xs_ref[:, ns + j * S5_SW:ns + (j + 1) * S5_SW]], axis=-1)
            y = _dot(s_cat.astype(BF16), wc_ref[j]) + dsk_ref[:, cs] * u_ref[rows, cs]
            y_ref[rows, cs] = jax.nn.gelu(y).astype(BF16)

    gl = _dot(_dot(permt_ref[...], y_ref[...]).astype(BF16), wout_ref[...])
    o_ref[...] = (x + gl[:, :d] * jax.nn.sigmoid(gl[:, d:])).reshape(o_ref.shape)


def _s5_call(x3, s0, w, *, nb, nt, nsub, block, n_blocks):
    d = x3.shape[-1]
    r = nb * nt
    ns2 = s0.shape[-1]
    assert block[0] * block[1] == r and nt % nsub == 0 and nb % SUBLANES == 0 and d % S5_CW == 0
    assert block[1] % SUBLANES == 0
    perm = np.zeros((r, r), np.float32)
    bb, tt = np.meshgrid(np.arange(nb), np.arange(nt), indexing='ij')
    perm[(tt * nb + bb).ravel(), (bb * nt + tt).ravel()] = 1.0
    kern = functools.partial(_s5_kernel, nb=nb, nt=nt, nsub=nsub)
    consts = [s0, jnp.asarray(perm, BF16), jnp.asarray(perm.T, BF16),
              w['g'], w['w_in'], w['wb'], w['lam'], w['wc'], w['dsk'], w['w_out']]
    x_spec = pl.BlockSpec(tuple(block) + (d,), lambda i: (0, i, 0))
    return pl.pallas_call(
        kern,
        grid=(n_blocks,),
        in_specs=[x_spec] + [_const_spec(c.shape) for c in consts],
        out_specs=[x_spec, _const_spec((nb, ns2))],
        out_shape=[jax.ShapeDtypeStruct(x3.shape, F32), jax.ShapeDtypeStruct((nb, ns2), F32)],
        scratch_shapes=[
            pltpu.VMEM((r, d), F32),
            pltpu.VMEM((r // nsub, ns2), F32),
            pltpu.VMEM((r, d), BF16),
        ],
        compiler_params=_params("arbitrary"),
        name="s5_mixer",
    )(x3, *consts)


def _s5_weights(g, w_in, log_dt, a_re, a_im, b_re, b_im, c_re, c_im, d_skip, w_out):
    d = w_in.shape[0]
    ng, p, c = b_re.shape
    lr, li, bbr, bbi = _s5_disc_call(log_dt, a_re, a_im, b_re, b_im)
    eye = jnp.eye(S5_GB, dtype=F32)
    nblk = ng // S5_GB

    def b_tiles(bb):
        t = bb.reshape(nblk, S5_GB, p, c).transpose(0, 1, 3, 2)
        return jnp.einsum('ab,jacp->jacbp', eye, t).reshape(nblk, S5_CW, S5_SW)

    def c_tiles(cc):
        t = cc.reshape(nblk, S5_GB, c, p)
        return jnp.einsum('ba,jacp->jbpac', eye, t).reshape(nblk, S5_SW, S5_CW)

    lam = jnp.concatenate([lr.reshape(1, -1), li.reshape(1, -1)], axis=-1)
    return {
        'g': g.reshape(1, d),
        'w_in': w_in.astype(BF16),
        'wb': jnp.concatenate([b_tiles(bbr), b_tiles(bbi)], axis=-1).astype(BF16),
        'lam': jnp.broadcast_to(lam, (SUBLANES, lam.shape[-1])),
        'wc': jnp.concatenate([c_tiles(c_re), -c_tiles(c_im)], axis=1).astype(BF16),
        'dsk': d_skip.reshape(1, d),
        'w_out': w_out.astype(BF16),
    }


PROMPT_TM = 512
FLASH_TQ = 512
FLASH_QS = 512
FFN_TM = 512
S5_ROWS = 512
S5_NSUB = 2


def _mla_prompt_layer(x, w):
    b, s, d = x.shape
    tabs = _rope_tables(jnp.arange(s))
    q, k, v, ckv, kpe = _mla_proj_call(x, tabs, w, tm=min(PROMPT_TM, s))
    y = _flash_call(q, k, v, x, w['w_out'], tq=min(FLASH_TQ, s), qs=min(FLASH_QS, s))
    return y, ckv, kpe


def _mla_sample_layer(x, page_table, cache_ckv, cache_kpe, w, *, layer):
    b, t, d = x.shape
    past = page_table.shape[1] * PAGE_SIZE
    tabs = [jnp.tile(tb, (b, 1)) for tb in _rope_tables(past + jnp.arange(t))]
    q, _, _, ckv, kpe = _mla_proj_call(x.reshape(1, b * t, d), tabs, w, tm=b * t)
    ckv, kpe = ckv.reshape(b, t, KV_LORA), kpe.reshape(b, t, QK_ROPE)
    pad = lambda a: jnp.pad(a, ((0, 0), (0, NEW_PAD - t), (0, 0)))
    o = _paged_call(page_table, q.reshape(b, t, QK_PAD).astype(F32), pad(ckv), jnp.swapaxes(pad(kpe), 1, 2),
                    cache_ckv, jnp.swapaxes(cache_kpe, 2, 3), _paged_weights(w), layer=layer)
    y = _out_proj_call(x.reshape(b * t, d), o.reshape(b * t, QK_PAD), w['w_out'])
    return y.reshape(b, t, d), ckv, kpe


def _s5_layer(x, s0_re, s0_im, w):
    b, t, d = x.shape
    nt = min(S5_ROWS // b, t)
    s0 = jnp.concatenate([s0_re.reshape(b, -1), s0_im.reshape(b, -1)], axis=-1)
    if nt % SUBLANES == 0:
        y, st = _s5_call(x, s0, w, nb=b, nt=nt, nsub=S5_NSUB, block=(b, nt), n_blocks=t // nt)
    else:
        assert nt == t
        x3 = x.reshape(b * t // SUBLANES, SUBLANES, d)
        y, st = _s5_call(x3, s0, w, nb=b, nt=nt, nsub=S5_NSUB, block=x3.shape[:2], n_blocks=1)
        y = y.reshape(b, t, d)
    ns = st.shape[-1] // 2
    return y, st[:, :ns].reshape(s0_re.shape), st[:, ns:].reshape(s0_im.shape)


def _ffn_prompt_layer(x, w):
    b, s, d = x.shape
    zero = jnp.zeros((b, CONV_W - 1, w[1].shape[-1]), F32)
    return _ffn_call(x, zero, *w, nb=1, tm=min(FFN_TM, s))


def _ffn_sample_layer(x, prev, w):
    b, t, d = x.shape
    x_tm = jnp.swapaxes(x, 0, 1).reshape(1, t * b, d)
    prev_tm = jnp.swapaxes(prev, 0, 1).reshape(1, (CONV_W - 1) * b, prev.shape[-1])
    y, st = _ffn_call(x_tm, prev_tm, *w, nb=b, tm=t * b)
    return jnp.swapaxes(y.reshape(t, b, d), 0, 1), jnp.swapaxes(st.reshape(CONV_W - 1, b, -1), 0, 1)


def kernel(x_prompt, x_sample, cache_ckv, cache_kpe, state_ssm_re, state_ssm_im, state_conv, page_table,
           attn_norm, attn_w_in, attn_q_lat_norm, attn_w_q_up, attn_kv_lat_norm, attn_w_kv_up,
           attn_q_norm, attn_k_norm, attn_w_out,
           ssm_norm, ssm_w_in, ssm_log_dt, ssm_a_re, ssm_a_im, ssm_b_re, ssm_b_im,
           ssm_c_re, ssm_c_im, ssm_d, ssm_w_out,
           ffn_norm, ffn_w_up, ffn_conv_w, ffn_conv_b, ffn_w_down):
    xp, xs = x_prompt, x_sample
    depth = ffn_w_up.shape[0]
    bp = xp.shape[0]
    ssm_zero = jnp.zeros((bp,) + state_ssm_re.shape[2:], F32)
    ckv_p, kpe_p, ckv_s, kpe_s = [], [], [], []
    sre_p, sim_p, sre_s, sim_s = [], [], [], []
    conv_p, conv_s = [], []
    for i in range(depth):
        j = i // 2
        if i % 2 == 0:
            w = _mla_weights(attn_norm[j], attn_w_in[j], attn_q_lat_norm[j], attn_w_q_up[j], attn_kv_lat_norm[j],
                             attn_w_kv_up[j], attn_q_norm[j], attn_k_norm[j], attn_w_out[j])
            xp, c_p, r_p = _mla_prompt_layer(xp, w)
            xs, c_s, r_s = _mla_sample_layer(xs, page_table, cache_ckv, cache_kpe, w, layer=j)
            ckv_p.append(c_p)
            kpe_p.append(r_p)
            ckv_s.append(c_s)
            kpe_s.append(r_s)
        else:
            w = _s5_weights(ssm_norm[j], ssm_w_in[j], ssm_log_dt[j], ssm_a_re[j], ssm_a_im[j], ssm_b_re[j],
                            ssm_b_im[j], ssm_c_re[j], ssm_c_im[j], ssm_d[j], ssm_w_out[j])
            xp, r_p, i_p = _s5_layer(xp, ssm_zero, ssm_zero, w)
            xs, r_s, i_s = _s5_layer(xs, state_ssm_re[j], state_ssm_im[j], w)
            sre_p.append(r_p)
            sim_p.append(i_p)
            sre_s.append(r_s)
            sim_s.append(i_s)
        fw = (ffn_norm[i], ffn_w_up[i].astype(BF16), ffn_conv_w[i], ffn_conv_b[i], ffn_w_down[i].astype(BF16))
        xp, cv_p = _ffn_prompt_layer(xp, fw)
        xs, cv_s = _ffn_sample_layer(xs, state_conv[i], fw)
        conv_p.append(cv_p)
        conv_s.append(cv_s)
    return (xp, xs,
            jnp.stack(ckv_p), jnp.stack(kpe_p), jnp.stack(sre_p), jnp.stack(sim_p), jnp.stack(conv_p),
            jnp.stack(ckv_s), jnp.stack(kpe_s), jnp.stack(sre_s), jnp.stack(sim_s), jnp.stack(conv_s))
```

```python
import functools
import math

import jax
import jax.numpy as jnp
import numpy as np
from jax import lax
from jax.experimental import pallas as pl
from jax.experimental.pallas import tpu as pltpu

LANES = 128
SUBLANES = 8
VMEM_LIMIT_BYTES = 56 * 1024 * 1024

N_HEADS = 8
QK_NOPE = 64
QK_ROPE = 32
QK_HEAD = QK_NOPE + QK_ROPE
V_HEAD = 64
HEAD_PAD = LANES
QK_PAD = N_HEADS * HEAD_PAD
Q_LORA = 384
KV_LORA = 256
PAGE_SIZE = 128
ROPE_THETA = 10000.0
SSM_GROUP = 16
SSM_STATE = 64
CONV_W = 3
NORM_EPS = 1e-6

BF16 = jnp.bfloat16
F32 = jnp.float32


def _params(*sem):
    return pltpu.CompilerParams(dimension_semantics=sem, vmem_limit_bytes=VMEM_LIMIT_BYTES)


def _dot(a, b):
    return jnp.dot(a, b, preferred_element_type=F32)


def _dot_nt(a, b):
    return lax.dot_general(a, b, (((1,), (1,)), ((), ())), preferred_element_type=F32)


def _dot_tn(a, b):
    return lax.dot_general(a, b, (((0,), (0,)), ((), ())), preferred_element_type=F32)


def _rms(x, g):
    return x * lax.rsqrt(jnp.mean(x * x, axis=-1, keepdims=True) + NORM_EPS) * g


def _const_spec(shape, single=False):
    n = len(shape)
    if single:
        return pl.BlockSpec(shape, lambda *_: (0,) * n, pipeline_mode=pl.Buffered(1))
    return pl.BlockSpec(shape, lambda *_: (0,) * n)


FFN_CHUNK = 256

def _ffn_kernel(x_ref, prev_ref, g_ref, wup_ref, cw_ref, cb_ref, wdn_ref, o_ref, st_ref, h_ref, uc_ref, a_ref,
                *, nb, tm, off, d_ff, nc):
    hist = (CONV_W - 1) * nb

    @pl.when(pl.program_id(1) == 0)
    def _():
        st_ref[...] = prev_ref[...]

    x = x_ref[...]
    h_ref[...] = _rms(x, g_ref[...]).astype(BF16)

    for ci, c0 in enumerate(range(0, d_ff, nc)):
        def conv(half, lo):
            buf = uc_ref.at[2 * (ci % 2) + half]
            u = _dot(h_ref[...], wup_ref[:, lo:lo + nc])
            buf[off - hist:off, :] = st_ref[:, lo:lo + nc]
            buf[off:off + tm, :] = u
            st_ref[:, lo:lo + nc] = u[tm - hist:, :]
            c = cb_ref[:, lo:lo + nc]
            for j in range(CONV_W - 1):
                r0 = off - (CONV_W - 1 - j) * nb
                c = c + cw_ref[j:j + 1, lo:lo + nc] * buf[r0:r0 + tm, :]
            return c + cw_ref[CONV_W - 1:CONV_W, lo:lo + nc] * u
        val = conv(0, c0)
        gate = conv(1, d_ff + c0)
        a_ref[:, c0:c0 + nc] = (jax.nn.silu(gate) * val).astype(BF16)

    o_ref[...] = x + _dot(a_ref[...], wdn_ref[...])


def _ffn_call(x, prev, g, w_up, conv_w, conv_b, w_down, *, nb, tm):
    nseq, rows, d = x.shape
    d_ff = w_down.shape[0]
    hist = (CONV_W - 1) * nb
    off = -(-hist // SUBLANES) * SUBLANES
    nc = FFN_CHUNK
    assert rows % tm == 0 and tm % nb == 0 and d_ff % nc == 0 and tm >= hist
    kern = functools.partial(_ffn_kernel, nb=nb, tm=tm, off=off, d_ff=d_ff, nc=nc)
    return pl.pallas_call(
        kern,
        grid=(nseq, rows // tm),
        in_specs=[
            pl.BlockSpec((None, tm, d), lambda s, i: (s, i, 0)),
            pl.BlockSpec((None, hist, 2 * d_ff), lambda s, i: (s, 0, 0)),
            _const_spec((1, d)),
            _const_spec((d, 2 * d_ff), single=True),
            _const_spec((CONV_W, 2 * d_ff)),
            _const_spec((1, 2 * d_ff)),
            _const_spec((d_ff, d), single=True),
        ],
        out_specs=[
            pl.BlockSpec((None, tm, d), lambda s, i: (s, i, 0)),
            pl.BlockSpec((None, hist, 2 * d_ff), lambda s, i: (s, 0, 0)),
        ],
        out_shape=[
            jax.ShapeDtypeStruct((nseq, rows, d), F32),
            jax.ShapeDtypeStruct((nseq, hist, 2 * d_ff), F32),
        ],
        scratch_shapes=[
            pltpu.VMEM((tm, d), BF16),
            pltpu.VMEM((4, off + tm, nc), F32),
            pltpu.VMEM((tm, d_ff), BF16),
        ],
        compiler_params=_params("arbitrary", "arbitrary"),
        name="conv_ffn",
    )(x, prev, g.reshape(1, d), w_up, conv_w, conv_b.reshape(1, 2 * d_ff), w_down)


def _rope128(x, c, sa, sb):
    n = x.shape[-1]
    return x * c + pltpu.roll(x, n - QK_ROPE // 2, 1) * sa + pltpu.roll(x, QK_ROPE // 2, 1) * sb


def _head_ms(x, ones2):
    sq = (x * x).astype(BF16)
    parts = [_dot(sq[:, c0:c0 + 2 * HEAD_PAD], ones2) for c0 in range(0, x.shape[-1], 2 * HEAD_PAD)]
    return jnp.concatenate(parts, axis=-1) * (1.0 / QK_HEAD)


def _mla_proj_kernel(x_ref, tc_ref, tsa_ref, tsb_ref, gin_ref, win_ref, gql_ref, wq_ref, gkvl_ref, wkv_ref,
                     gq_ref, gk_ref, ones2_ref,
                     q_ref, k_ref, v_ref, ckv_ref, kpe_ref):
    x = x_ref[...]
    h = _rms(x, gin_ref[...]).astype(BF16)
    proj = _dot(h, win_ref[...])
    cq = proj[:, :Q_LORA]
    ckv = proj[:, Q_LORA:Q_LORA + KV_LORA]
    kp = proj[:, Q_LORA + KV_LORA:]
    tc, tsa, tsb = tc_ref[...], tsa_ref[...], tsb_ref[...]
    ones2 = ones2_ref[...]

    q = _dot(_rms(cq, gql_ref[...]).astype(BF16), wq_ref[...])
    rep = lambda t: jnp.concatenate([t] * N_HEADS, axis=-1)
    q = _rope128(q, rep(tc), rep(tsa), rep(tsb))
    q = q * lax.rsqrt(_head_ms(q, ones2) + NORM_EPS) * gq_ref[...]
    q_ref[...] = q.astype(q_ref.dtype)

    ckv_n = _rms(ckv, gkvl_ref[...])
    ckv_ref[...] = ckv_n
    kp = _rope128(kp, tc, tsa, tsb)
    kpe_ref[...] = kp[:, QK_NOPE:QK_HEAD]

    kv = _dot(ckv_n.astype(BF16), wkv_ref[...])
    k = kv[:, :QK_PAD] + rep(kp)
    k = k * lax.rsqrt(_head_ms(k, ones2) + NORM_EPS) * gk_ref[...]
    k_ref[...] = k.astype(k_ref.dtype)
    v_ref[...] = kv[:, QK_PAD:].astype(v_ref.dtype)


def _mla_proj_call(x, tabs, w, *, tm):
    nseq, rows, d = x.shape
    assert rows % tm == 0
    row_spec = lambda n: pl.BlockSpec((None, tm, n), lambda s, i: (s, i, 0))
    tab_spec = pl.BlockSpec((tm, LANES), lambda s, i: (i, 0))
    consts = [w['g_in'], w['w_in'], w['g_ql'], w['w_q'], w['g_kvl'], w['w_kv'], w['g_q'], w['g_k'], w['ones2']]
    return pl.pallas_call(
        _mla_proj_kernel,
        grid=(nseq, rows // tm),
        in_specs=[row_spec(d), tab_spec, tab_spec, tab_spec] + [_const_spec(c.shape) for c in consts],
        out_specs=[row_spec(QK_PAD), row_spec(QK_PAD), row_spec(QK_PAD), row_spec(KV_LORA), row_spec(QK_ROPE)],
        out_shape=[
            jax.ShapeDtypeStruct((nseq, rows, QK_PAD), BF16),
            jax.ShapeDtypeStruct((nseq, rows, QK_PAD), BF16),
            jax.ShapeDtypeStruct((nseq, rows, QK_PAD), BF16),
            jax.ShapeDtypeStruct((nseq, rows, KV_LORA), F32),
            jax.ShapeDtypeStruct((nseq, rows, QK_ROPE), F32),
        ],
        compiler_params=_params("arbitrary", "arbitrary"),
        name="mla_proj",
    )(x, *tabs, *consts)


def _head_pad_cols(w, per_head, lo, hi):
    k = w.shape[0]
    w = w.reshape(k, N_HEADS, per_head)[:, :, lo:hi]
    w = jnp.pad(w, ((0, 0), (0, 0), (0, HEAD_PAD - (hi - lo))))
    return w.reshape(k, QK_PAD)


def _head_pad_vec(g):
    return jnp.tile(jnp.pad(g, (0, HEAD_PAD - g.shape[0])), N_HEADS).reshape(1, QK_PAD)


def _rope_tables(pos):
    half = QK_ROPE // 2
    inv = ROPE_THETA ** (-jnp.arange(half, dtype=F32) * (2.0 / QK_ROPE))
    ang = pos.astype(F32)[:, None] * inv[None, :]
    cos, sin = jnp.cos(ang), jnp.sin(ang)
    t = pos.shape[0]
    z = lambda n: jnp.zeros((t, n), F32)
    c = jnp.concatenate([jnp.ones((t, QK_NOPE), F32), cos, cos, z(HEAD_PAD - QK_HEAD)], axis=-1)
    sa = jnp.concatenate([z(QK_NOPE), -sin, z(half), z(HEAD_PAD - QK_HEAD)], axis=-1)
    sb = jnp.concatenate([z(QK_NOPE), z(half), sin, z(HEAD_PAD - QK_HEAD)], axis=-1)
    return c, sa, sb


def _mla_weights(g_in, w_in, g_ql, w_q_up, g_kvl, w_kv_up, g_qn, g_kn, w_out):
    d = w_in.shape[0]
    w_kpe = jnp.pad(w_in[:, Q_LORA + KV_LORA:], ((0, 0), (QK_NOPE, HEAD_PAD - QK_HEAD)))
    w_in_ext = jnp.concatenate([w_in[:, :Q_LORA + KV_LORA], w_kpe], axis=1)
    w_uk = _head_pad_cols(w_kv_up, QK_NOPE + V_HEAD, 0, QK_NOPE)
    w_uv = _head_pad_cols(w_kv_up, QK_NOPE + V_HEAD, QK_NOPE, QK_NOPE + V_HEAD)
    blk = np.kron(np.eye(2, dtype=np.float32), np.ones((HEAD_PAD, HEAD_PAD), np.float32))
    w_out_pad = jnp.pad(w_out.reshape(N_HEADS, V_HEAD, d), ((0, 0), (0, HEAD_PAD - V_HEAD), (0, 0)))
    return {
        'g_in': g_in.reshape(1, d),
        'w_in': w_in_ext.astype(BF16),
        'g_ql': g_ql.reshape(1, Q_LORA),
        'w_q': _head_pad_cols(w_q_up, QK_HEAD, 0, QK_HEAD).astype(BF16),
        'g_kvl': g_kvl.reshape(1, KV_LORA),
        'w_kv': jnp.concatenate([w_uk, w_uv], axis=1).astype(BF16),
        'g_q': _head_pad_vec(g_qn) * (QK_HEAD ** -0.5 * math.log2(math.e)),
        'g_k': _head_pad_vec(g_kn),
        'ones2': jnp.asarray(blk, BF16),
        'w_out': w_out_pad.reshape(QK_PAD, d).astype(BF16),
        'w_uk': w_uk, 'w_uv': w_uv, 'g_kn': g_kn,
    }


def _flash_kernel(q_ref, k_ref, v_ref, x_ref, wo_ref, o_ref, m_ref, l_ref, acc_ref, *, tq, qs):
    tk = tq
    i = pl.program_id(1)
    j = pl.program_id(2)

    @pl.when(j == 0)
    def _():
        m_ref[...] = jnp.full(m_ref.shape, -jnp.inf, F32)
        l_ref[...] = jnp.zeros(l_ref.shape, F32)
        acc_ref[...] = jnp.zeros(acc_ref.shape, F32)

    def step(diagonal):
        keep = {}
        if diagonal:
            for r0 in range(0, tq, qs):
                row = r0 + lax.broadcasted_iota(jnp.int32, (qs, r0 + qs), 0)
                keep[r0] = lax.broadcasted_iota(jnp.int32, (qs, r0 + qs), 1) <= row
        for h in range(N_HEADS):
            hs = slice(h * HEAD_PAD, (h + 1) * HEAD_PAD)
            for r0 in range(0, tq, qs):
                rs = slice(r0, r0 + qs)
                nk = r0 + qs if diagonal else tk
                s = _dot_nt(q_ref[rs, hs], k_ref[:nk, hs])
                if diagonal:
                    s = jnp.where(keep[r0], s, jnp.finfo(F32).min)
                m_old = m_ref[h, rs]
                m_new = jnp.maximum(m_old, jnp.max(s, axis=-1, keepdims=True))
                alpha = jnp.exp2(m_old - m_new)
                p = jnp.exp2(s - m_new[:, :1])
                l_ref[h, rs] = alpha * l_ref[h, rs] + jnp.sum(p, axis=-1, keepdims=True)
                acc_ref[h, rs] = alpha * acc_ref[h, rs] + _dot(p.astype(BF16), v_ref[:nk, hs])
                m_ref[h, rs] = m_new

    @pl.when(j < i)
    def _():
        step(False)

    @pl.when(j == i)
    def _():
        step(True)
        o = jnp.concatenate([acc_ref[h] / l_ref[h] for h in range(N_HEADS)], axis=-1)
        o_ref[...] = x_ref[...] + _dot(o.astype(BF16), wo_ref[...])


def _flash_call(q, k, v, x, w_out_pad, *, tq, qs):
    nseq, s, d = x.shape
    tk = tq
    assert s % tq == 0 and tq % qs == 0
    kern = functools.partial(_flash_kernel, tq=tq, qs=qs)
    kv_map = lambda b, i, j: (b, jnp.minimum(j, i), 0)
    return pl.pallas_call(
        kern,
        grid=(nseq, s // tq, s // tk),
        in_specs=[
            pl.BlockSpec((None, tq, QK_PAD), lambda b, i, j: (b, i, 0)),
            pl.BlockSpec((None, tk, QK_PAD), kv_map),
            pl.BlockSpec((None, tk, QK_PAD), kv_map),
            pl.BlockSpec((None, tq, d), lambda b, i, j: (b, i, 0)),
            _const_spec((QK_PAD, d)),
        ],
        out_specs=pl.BlockSpec((None, tq, d), lambda b, i, j: (b, i, 0)),
        out_shape=jax.ShapeDtypeStruct((nseq, s, d), F32),
        scratch_shapes=[
            pltpu.VMEM((N_HEADS, tq, HEAD_PAD), F32),
            pltpu.VMEM((N_HEADS, tq, HEAD_PAD), F32),
            pltpu.VMEM((N_HEADS, tq, HEAD_PAD), F32),
        ],
        compiler_params=_params("arbitrary", "arbitrary", "arbitrary"),
        name="mla_flash",
    )(q, k, v, x, w_out_pad)


NEW_PAD = LANES
KEY_TILE = 1024


def _paged_kernel(pt_ref, q_ref, cnew_ref, rnew_ref, ckv_hbm, kpe_hbm, wukt_ref, wukr_ref,
                  gkr_ref, wuv_ref, o_ref, cbuf, rbuf, s_ref, cbf_ref, csem, rsem, *, layer, n_pages, n_new):
    b = pl.program_id(0)
    nb = pl.num_programs(0)
    slot = b % 2
    past = n_pages * PAGE_SIZE
    n_tiles = past // KEY_TILE
    ppt = KEY_TILE // PAGE_SIZE

    def start_tile(bb, sl, i):
        for k in range(ppt):
            p = i * ppt + k
            pg = pt_ref[bb, p]
            rows = pl.ds(pl.multiple_of(p * PAGE_SIZE, PAGE_SIZE), PAGE_SIZE)
            pltpu.make_async_copy(ckv_hbm.at[layer, pg], cbuf.at[sl, rows, :], csem.at[sl]).start()
            pltpu.make_async_copy(kpe_hbm.at[layer, pg], rbuf.at[sl, i, :, pl.ds(k * PAGE_SIZE, PAGE_SIZE)],
                                  rsem.at[sl]).start()

    @pl.when(b == 0)
    def _():
        def body(i, c):
            start_tile(0, 0, i)
            return c
        lax.fori_loop(0, n_tiles, body, 0)

    pltpu.make_async_copy(cbuf.at[slot], cbuf.at[slot], csem.at[slot]).wait()
    pltpu.make_async_copy(rbuf.at[slot], rbuf.at[slot], rsem.at[slot]).wait()

    nq = n_new * N_HEADS
    q = q_ref[...]
    qb = jnp.concatenate([jnp.broadcast_to(q[t:t + 1], (N_HEADS, QK_PAD)) for t in range(n_new)], axis=0)
    row_head = lax.broadcasted_iota(jnp.int32, (nq, QK_PAD), 0) % N_HEADS
    lane_head = lax.broadcasted_iota(jnp.int32, (nq, QK_PAD), 1) // HEAD_PAD
    own_head = row_head == lane_head
    qb = jnp.where(own_head, qb, 0.0)
    qlat = _dot(qb.astype(BF16), wukt_ref[...]).astype(BF16)
    wcat = jnp.concatenate([wukr_ref[...], qlat], axis=0)
    qsum = qb[:, :HEAD_PAD]
    for h in range(1, N_HEADS):
        qsum = qsum + qb[:, h * HEAD_PAD:(h + 1) * HEAD_PAD]
    qpe = (qsum[:, QK_NOPE:QK_HEAD] * gkr_ref[...]).astype(BF16)
    n_kn = N_HEADS * QK_NOPE

    def scores(ckv_t, kpe_tt):
        nk = ckv_t.shape[0]
        cb = ckv_t.astype(BF16)
        a = _dot_nt(wcat, cb)
        kn = a[:n_kn]
        ssq = jnp.sum((kn * kn).reshape(QK_NOPE, N_HEADS, nk), axis=0)
        ssq = ssq + jnp.sum(kpe_tt * kpe_tt, axis=0, keepdims=True)
        r = lax.rsqrt(ssq * (1.0 / QK_HEAD) + NORM_EPS)
        s = a[n_kn:] + _dot(qpe, kpe_tt.astype(BF16))
        return s * jnp.concatenate([r] * n_new, axis=0), cb

    nxt = jnp.minimum(b + 1, nb - 1)

    def tile_rows(i):
        return pl.ds(pl.multiple_of(i * KEY_TILE, KEY_TILE), KEY_TILE)

    def pass1(i, m):
        start_tile(nxt, 1 - slot, i)
        rows = tile_rows(i)
        s, cb = scores(cbuf[slot, rows, :], rbuf[slot, i])
        s_ref[i] = s
        cbf_ref[rows, :] = cb
        return jnp.maximum(m, jnp.max(s, axis=1, keepdims=True))

    m = lax.fori_loop(0, n_tiles, pass1, jnp.full((nq, 1), -jnp.inf, F32), unroll=2)

    @pl.when(b == nb - 1)
    def _():
        pltpu.make_async_copy(cbuf.at[1 - slot], cbuf.at[1 - slot], csem.at[1 - slot]).wait()
        pltpu.make_async_copy(rbuf.at[1 - slot], rbuf.at[1 - slot], rsem.at[1 - slot]).wait()

    s_new, cb_new = scores(cnew_ref[...], rnew_ref[...])
    key = lax.broadcasted_iota(jnp.int32, (nq, NEW_PAD), 1)
    qt = lax.broadcasted_iota(jnp.int32, (nq, NEW_PAD), 0) // N_HEADS
    s_new = jnp.where((key < n_new) & (key <= qt), s_new, jnp.finfo(F32).min)
    m = jnp.maximum(m, jnp.max(s_new, axis=1, keepdims=True))

    p_new = jnp.exp2(s_new - m)
    l0 = jnp.sum(p_new, axis=1, keepdims=True)
    acc0 = _dot(p_new.astype(BF16), cb_new)

    def pass2(i, carry):
        l, acc = carry
        p = jnp.exp2(s_ref[i] - m)
        return l + jnp.sum(p, axis=1, keepdims=True), acc + _dot(p.astype(BF16), cbf_ref[tile_rows(i), :])

    l, o_lat = lax.fori_loop(0, n_tiles, pass2, (l0, acc0), unroll=2)

    o = _dot((o_lat / l).astype(BF16), wuv_ref[...])
    o = jnp.where(own_head, o, 0.0)
    o_ref[...] = jnp.sum(o.reshape(n_new, N_HEADS, QK_PAD), axis=1)


def _paged_call(page_table, q, ckv_new, kpe_new, cache_ckv, cache_kpe, w, *, layer):
    nb, n_new, _ = q.shape
    n_pages = page_table.shape[1]
    past = n_pages * PAGE_SIZE
    assert past % KEY_TILE == 0 and n_new <= NEW_PAD
    nq = n_new * N_HEADS
    kern = functools.partial(_paged_kernel, layer=layer, n_pages=n_pages, n_new=n_new)
    consts = [w['w_ukt'], w['w_uk_r'], w['g_kr'], w['w_uv_p']]
    seq = lambda n1, n2: pl.BlockSpec((None, n1, n2), lambda b, pt: (b, 0, 0))
    cspec = lambda c: pl.BlockSpec(c.shape, lambda b, pt: (0,) * c.ndim)
    return pl.pallas_call(
        kern,
        grid_spec=pltpu.PrefetchScalarGridSpec(
            num_scalar_prefetch=1,
            grid=(nb,),
            in_specs=[seq(n_new, QK_PAD), seq(NEW_PAD, KV_LORA), seq(QK_ROPE, NEW_PAD),
                      pl.BlockSpec(memory_space=pl.ANY), pl.BlockSpec(memory_space=pl.ANY)]
                     + [cspec(c) for c in consts],
            out_specs=seq(n_new, QK_PAD),
            scratch_shapes=[
                pltpu.VMEM((2, past, KV_LORA), F32),
                pltpu.VMEM((2, past // KEY_TILE, QK_ROPE, KEY_TILE), F32),
                pltpu.VMEM((past // KEY_TILE, nq, KEY_TILE), F32),
                pltpu.VMEM((past, KV_LORA), BF16),
                pltpu.SemaphoreType.DMA((2,)),
                pltpu.SemaphoreType.DMA((2,)),
            ],
        ),
        out_shape=jax.ShapeDtypeStruct((nb, n_new, QK_PAD), F32),
        compiler_params=_params("arbitrary"),
        name="mla_paged",
    )(page_table, q, ckv_new, kpe_new, cache_ckv, cache_kpe, *consts)


def _paged_weights(w):
    g_kn = w['g_kn']
    g_pad = jnp.tile(jnp.pad(g_kn[:QK_NOPE], (0, HEAD_PAD - QK_NOPE)), N_HEADS)
    w_uk_r = w['w_uk'].reshape(KV_LORA, N_HEADS, HEAD_PAD)[:, :, :QK_NOPE].transpose(2, 1, 0)
    return {
        'w_ukt': (w['w_uk'].T * g_pad[:, None]).astype(BF16),
        'w_uk_r': w_uk_r.reshape(QK_NOPE * N_HEADS, KV_LORA).astype(BF16),
        'g_kr': g_kn[QK_NOPE:].reshape(1, QK_ROPE),
        'w_uv_p': w['w_uv'].astype(BF16),
    }


def _out_proj_kernel(x_ref, o_ref, w_ref, y_ref):
    y_ref[...] = x_ref[...] + _dot(o_ref[...].astype(BF16), w_ref[...])


def _out_proj_call(x, o, w_out_pad):
    return pl.pallas_call(
        _out_proj_kernel,
        out_shape=jax.ShapeDtypeStruct(x.shape, F32),
        compiler_params=pltpu.CompilerParams(vmem_limit_bytes=VMEM_LIMIT_BYTES),
        name="mla_out_proj",
    )(x, o, w_out_pad)


def _s5_lambda(ldt, ar, ai):
    dt = jnp.exp(ldt)
    mag = jnp.exp(ar * dt)
    lr = mag * jnp.cos(ai * dt)
    li = mag * jnp.sin(ai * dt)
    den = ar * ar + ai * ai
    cr = ((lr - 1.0) * ar + li * ai) / den
    ci = (li * ar - (lr - 1.0) * ai) / den
    return lr, li, cr, ci


def _s5_disc_kernel(ldt_ref, ar_ref, ai_ref, ldte_ref, are_ref, aie_ref, br_ref, bi_ref,
                    lr_ref, li_ref, bbr_ref, bbi_ref):
    lr, li, _, _ = _s5_lambda(ldt_ref[...], ar_ref[...], ai_ref[...])
    lr_ref[...] = lr
    li_ref[...] = li
    _, _, cr, ci = _s5_lambda(ldte_ref[...], are_ref[...], aie_ref[...])
    br, bi = br_ref[...], bi_ref[...]
    bbr_ref[...] = cr * br - ci * bi
    bbi_ref[...] = cr * bi + ci * br


def _s5_disc_call(log_dt, a_re, a_im, b_re, b_im):
    g, p, c = b_re.shape
    flat = lambda a: jnp.broadcast_to(a[:, :, None], (g, p, c)).reshape(-1, LANES)
    outs = pl.pallas_call(
        _s5_disc_kernel,
        out_shape=[jax.ShapeDtypeStruct((g, p), F32)] * 2 + [jax.ShapeDtypeStruct((g * p * c // LANES, LANES), F32)] * 2,
        name="s5_discretise",
    )(log_dt, a_re, a_im, flat(log_dt), flat(a_re), flat(a_im), b_re.reshape(-1, LANES), b_im.reshape(-1, LANES))
    lr, li, bbr, bbi = outs
    return lr, li, bbr.reshape(g, p, c), bbi.reshape(g, p, c)


S5_GB = 8
S5_CW = S5_GB * SSM_GROUP
S5_SW = S5_GB * SSM_STATE
S5_LC = 1024


def _s5_kernel(x_ref, s0_ref, perm_ref, permt_ref, g_ref, win_ref, wb_ref, lam_ref, wc_ref, dsk_ref, wout_ref,
               o_ref, st_ref, u_ref, xs_ref, y_ref, *, nb, nt, nsub):
    i = pl.program_id(0)
    d = x_ref.shape[-1]
    ns = st_ref.shape[-1] // 2
    nts = nt // nsub
    rs = nts * nb

    @pl.when(i == 0)
    def _():
        st_ref[...] = s0_ref[...]

    x = x_ref[...].reshape(nb * nt, d)
    h = _rms(x, g_ref[...]).astype(BF16)
    h = _dot(perm_ref[...], h).astype(BF16)
    u_ref[...] = _dot(h, win_ref[...])

    for sub in range(nsub):
        rows = slice(sub * rs, (sub + 1) * rs)
        xs = xs_ref.at[sub % 2]
        for j in range(d // S5_CW):
            xb = _dot(u_ref[rows, j * S5_CW:(j + 1) * S5_CW].astype(BF16), wb_ref[j])
            xs[:, j * S5_SW:(j + 1) * S5_SW] = xb[:, :S5_SW]
            xs[:, ns + j * S5_SW:ns + (j + 1) * S5_SW] = xb[:, S5_SW:]

        def scan_rows(b0):
            for lc in range(0, ns, S5_LC):
                re, im = slice(lc, lc + S5_LC), slice(ns + lc, ns + lc + S5_LC)
                sr, si = st_ref[pl.ds(b0, SUBLANES), re], st_ref[pl.ds(b0, SUBLANES), im]
                for t in range(nts):
                    r8 = pl.ds(t * nb + b0, SUBLANES)
                    lr, li = lam_ref[:, re], lam_ref[:, im]
                    sr, si = lr * sr - li * si + xs[r8, re], lr * si + li * sr + xs[r8, im]
                    xs[r8, re] = sr
                    xs[r8, im] = si
                st_ref[pl.ds(b0, SUBLANES), re] = sr
                st_ref[pl.ds(b0, SUBLANES), im] = si

        if nb == SUBLANES:
            scan_rows(0)
        else:
            def scan_tile(bt, carry):
                scan_rows(pl.multiple_of(bt * SUBLANES, SUBLANES))
                return carry
            lax.fori_loop(0, nb // SUBLANES, scan_tile, 0)

        for j in range(d // S5_CW):
            cs = slice(j * S5_CW, (j + 1) * S5_CW)
            s_cat = jnp.concatenate([xs[:, j * S5_SW:(j + 1) * S5_SW],
                                     xs[:, ns + j * S5_SW:ns + (j + 1) * S5_SW]], axis=-1)
            y = _dot(s_cat.astype(BF16), wc_ref[j]) + dsk_ref[:, cs] * u_ref[rows, cs]
            y_ref[rows, cs] = jax.nn.gelu(y).astype(BF16)

    gl = _dot(_dot(permt_ref[...], y_ref[...]).astype(BF16), wout_ref[...])
    o_ref[...] = (x + gl[:, :d] * jax.nn.sigmoid(gl[:, d:])).reshape(o_ref.shape)


def _s5_call(x3, s0, w, *, nb, nt, nsub, block, n_blocks):
    d = x3.shape[-1]
    r = nb * nt
    ns2 = s0.shape[-1]
    assert block[0] * block[1] == r and nt % nsub == 0 and nb % SUBLANES == 0 and d % S5_CW == 0
    assert block[1] % SUBLANES == 0
    perm = np.zeros((r, r), np.float32)
    bb, tt = np.meshgrid(np.arange(nb), np.arange(nt), indexing='ij')
    perm[(tt * nb + bb).ravel(), (bb * nt + tt).ravel()] = 1.0
    kern = functools.partial(_s5_kernel, nb=nb, nt=nt, nsub=nsub)
    consts = [s0, jnp.asarray(perm, BF16), jnp.asarray(perm.T, BF16),
              w['g'], w['w_in'], w['wb'], w['lam'], w['wc'], w['dsk'], w['w_out']]
    x_spec = pl.BlockSpec(tuple(block) + (d,), lambda i: (0, i, 0))
    return pl.pallas_call(
        kern,
        grid=(n_blocks,),
        in_specs=[x_spec] + [_const_spec(c.shape) for c in consts],
        out_specs=[x_spec, _const_spec((nb, ns2))],
        out_shape=[jax.ShapeDtypeStruct(x3.shape, F32), jax.ShapeDtypeStruct((nb, ns2), F32)],
        scratch_shapes=[
            pltpu.VMEM((r, d), F32),
            pltpu.VMEM((2, r // nsub, ns2), F32),
            pltpu.VMEM((r, d), BF16),
        ],
        compiler_params=_params("arbitrary"),
        name="s5_mixer",
    )(x3, *consts)


def _s5_weights(g, w_in, log_dt, a_re, a_im, b_re, b_im, c_re, c_im, d_skip, w_out):
    d = w_in.shape[0]
    ng, p, c = b_re.shape
    lr, li, bbr, bbi = _s5_disc_call(log_dt, a_re, a_im, b_re, b_im)
    eye = jnp.eye(S5_GB, dtype=F32)
    nblk = ng // S5_GB

    def b_tiles(bb):
        t = bb.reshape(nblk, S5_GB, p, c).transpose(0, 1, 3, 2)
        return jnp.einsum('ab,jacp->jacbp', eye, t).reshape(nblk, S5_CW, S5_SW)

    def c_tiles(cc):
        t = cc.reshape(nblk, S5_GB, c, p)
        return jnp.einsum('ba,jacp->jbpac', eye, t).reshape(nblk, S5_SW, S5_CW)

    lam = jnp.concatenate([lr.reshape(1, -1), li.reshape(1, -1)], axis=-1)
    return {
        'g': g.reshape(1, d),
        'w_in': w_in.astype(BF16),
        'wb': jnp.concatenate([b_tiles(bbr), b_tiles(bbi)], axis=-1).astype(BF16),
        'lam': jnp.broadcast_to(lam, (SUBLANES, lam.shape[-1])),
        'wc': jnp.concatenate([c_tiles(c_re), -c_tiles(c_im)], axis=1).astype(BF16),
        'dsk': d_skip.reshape(1, d),
        'w_out': w_out.astype(BF16),
    }


PROMPT_TM = 512
FLASH_TQ = 512
FLASH_QS = 512
FFN_TM = 1024
S5_ROWS = 512
S5_NSUB = 2


def _mla_prompt_layer(x, w):
    b, s, d = x.shape
    tabs = _rope_tables(jnp.arange(s))
    q, k, v, ckv, kpe = _mla_proj_call(x, tabs, w, tm=min(PROMPT_TM, s))
    y = _flash_call(q, k, v, x, w['w_out'], tq=min(FLASH_TQ, s), qs=min(FLASH_QS, s))
    return y, ckv, kpe


def _mla_sample_layer(x, page_table, cache_ckv, cache_kpe, w, *, layer):
    b, t, d = x.shape
    past = page_table.shape[1] * PAGE_SIZE
    tabs = [jnp.tile(tb, (b, 1)) for tb in _rope_tables(past + jnp.arange(t))]
    q, _, _, ckv, kpe = _mla_proj_call(x.reshape(1, b * t, d), tabs, w, tm=b * t)
    ckv, kpe = ckv.reshape(b, t, KV_LORA), kpe.reshape(b, t, QK_ROPE)
    pad = lambda a: jnp.pad(a, ((0, 0), (0, NEW_PAD - t), (0, 0)))
    o = _paged_call(page_table, q.reshape(b, t, QK_PAD).astype(F32), pad(ckv), jnp.swapaxes(pad(kpe), 1, 2),
                    cache_ckv, jnp.swapaxes(cache_kpe, 2, 3), _paged_weights(w), layer=layer)
    y = _out_proj_call(x.reshape(b * t, d), o.reshape(b * t, QK_PAD), w['w_out'])
    return y.reshape(b, t, d), ckv, kpe


def _s5_layer(x, s0_re, s0_im, w):
    b, t, d = x.shape
    nt = min(S5_ROWS // b, t)
    s0 = jnp.concatenate([s0_re.reshape(b, -1), s0_im.reshape(b, -1)], axis=-1)
    if nt % SUBLANES == 0:
        y, st = _s5_call(x, s0, w, nb=b, nt=nt, nsub=S5_NSUB, block=(b, nt), n_blocks=t // nt)
    else:
        assert nt == t
        x3 = x.reshape(b * t // SUBLANES, SUBLANES, d)
        y, st = _s5_call(x3, s0, w, nb=b, nt=nt, nsub=S5_NSUB, block=x3.shape[:2], n_blocks=1)
        y = y.reshape(b, t, d)
    ns = st.shape[-1] // 2
    return y, st[:, :ns].reshape(s0_re.shape), st[:, ns:].reshape(s0_im.shape)


def _ffn_prompt_layer(x, w):
    b, s, d = x.shape
    zero = jnp.zeros((b, CONV_W - 1, w[1].shape[-1]), F32)
    return _ffn_call(x, zero, *w, nb=1, tm=min(FFN_TM, s))


def _ffn_sample_layer(x, prev, w):
    b, t, d = x.shape
    x_tm = jnp.swapaxes(x, 0, 1).reshape(1, t * b, d)
    prev_tm = jnp.swapaxes(prev, 0, 1).reshape(1, (CONV_W - 1) * b, prev.shape[-1])
    y, st = _ffn_call(x_tm, prev_tm, *w, nb=b, tm=t * b)
    return jnp.swapaxes(y.reshape(t, b, d), 0, 1), jnp.swapaxes(st.reshape(CONV_W - 1, b, -1), 0, 1)


def kernel(x_prompt, x_sample, cache_ckv, cache_kpe, state_ssm_re, state_ssm_im, state_conv, page_table,
           attn_norm, attn_w_in, attn_q_lat_norm, attn_w_q_up, attn_kv_lat_norm, attn_w_kv_up,
           attn_q_norm, attn_k_norm, attn_w_out,
           ssm_norm, ssm_w_in, ssm_log_dt, ssm_a_re, ssm_a_im, ssm_b_re, ssm_b_im,
           ssm_c_re, ssm_c_im, ssm_d, ssm_w_out,
           ffn_norm, ffn_w_up, ffn_conv_w, ffn_conv_b, ffn_w_down):
    xp, xs = x_prompt, x_sample
    depth = ffn_w_up.shape[0]
    bp = xp.shape[0]
    ssm_zero = jnp.zeros((bp,) + state_ssm_re.shape[2:], F32)
    ckv_p, kpe_p, ckv_s, kpe_s = [], [], [], []
    sre_p, sim_p, sre_s, sim_s = [], [], [], []
    conv_p, conv_s = [], []
    for i in range(depth):
        j = i // 2
        if i % 2 == 0:
            w = _mla_weights(attn_norm[j], attn_w_in[j], attn_q_lat_norm[j], attn_w_q_up[j], attn_kv_lat_norm[j],
                             attn_w_kv_up[j], attn_q_norm[j], attn_k_norm[j], attn_w_out[j])
            xp, c_p, r_p = _mla_prompt_layer(xp, w)
            xs, c_s, r_s = _mla_sample_layer(xs, page_table, cache_ckv, cache_kpe, w, layer=j)
            ckv_p.append(c_p)
            kpe_p.append(r_p)
            ckv_s.append(c_s)
            kpe_s.append(r_s)
        else:
            w = _s5_weights(ssm_norm[j], ssm_w_in[j], ssm_log_dt[j], ssm_a_re[j], ssm_a_im[j], ssm_b_re[j],
                            ssm_b_im[j], ssm_c_re[j], ssm_c_im[j], ssm_d[j], ssm_w_out[j])
            xp, r_p, i_p = _s5_layer(xp, ssm_zero, ssm_zero, w)
            xs, r_s, i_s = _s5_layer(xs, state_ssm_re[j], state_ssm_im[j], w)
            sre_p.append(r_p)
            sim_p.append(i_p)
            sre_s.append(r_s)
            sim_s.append(i_s)
        fw = (ffn_norm[i], ffn_w_up[i].astype(BF16), ffn_conv_w[i], ffn_conv_b[i], ffn_w_down[i].astype(BF16))
        xp, cv_p = _ffn_prompt_layer(xp, fw)
        xs, cv_s = _ffn_sample_layer(xs, state_conv[i], fw)
        conv_p.append(cv_p)
        conv_s.append(cv_s)
    return (xp, xs,
            jnp.stack(ckv_p), jnp.stack(kpe_p), jnp.stack(sre_p), jnp.stack(sim_p), jnp.stack(conv_p),
            jnp.stack(ckv_s), jnp.stack(kpe_s), jnp.stack(sre_s), jnp.stack(sim_s), jnp.stack(conv_s))
```

```python
import functools
import math

import jax
import jax.numpy as jnp
import numpy as np
from jax import lax
from jax.experimental import pallas as pl
from jax.experimental.pallas import tpu as pltpu

LANES = 128
SUBLANES = 8
VMEM_LIMIT_BYTES = 56 * 1024 * 1024

N_HEADS = 8
QK_NOPE = 64
QK_ROPE = 32
QK_HEAD = QK_NOPE + QK_ROPE
V_HEAD = 64
HEAD_PAD = LANES
QK_PAD = N_HEADS * HEAD_PAD
Q_LORA = 384
KV_LORA = 256
PAGE_SIZE = 128
ROPE_THETA = 10000.0
SSM_GROUP = 16
SSM_STATE = 64
CONV_W = 3
NORM_EPS = 1e-6

BF16 = jnp.bfloat16
F32 = jnp.float32


def _params(*sem):
    return pltpu.CompilerParams(dimension_semantics=sem, vmem_limit_bytes=VMEM_LIMIT_BYTES)


def _dot(a, b):
    return jnp.dot(a, b, preferred_element_type=F32)


def _dot_nt(a, b):
    return lax.dot_general(a, b, (((1,), (1,)), ((), ())), preferred_element_type=F32)


def _dot_tn(a, b):
    return lax.dot_general(a, b, (((0,), (0,)), ((), ())), preferred_element_type=F32)


def _rms(x, g):
    return x * lax.rsqrt(jnp.mean(x * x, axis=-1, keepdims=True) + NORM_EPS) * g


def _const_spec(shape, single=False):
    n = len(shape)
    if single:
        return pl.BlockSpec(shape, lambda *_: (0,) * n, pipeline_mode=pl.Buffered(1))
    return pl.BlockSpec(shape, lambda *_: (0,) * n)


FFN_CHUNK = 256

def _ffn_kernel(x_ref, prev_ref, g_ref, wup_ref, cw_ref, cb_ref, wdn_ref, o_ref, st_ref, h_ref, uc_ref, a_ref,
                *, nb, tm, off, d_ff, nc):
    hist = (CONV_W - 1) * nb

    @pl.when(pl.program_id(1) == 0)
    def _():
        st_ref[...] = prev_ref[...]

    x = x_ref[...]
    h_ref[...] = _rms(x, g_ref[...]).astype(BF16)

    for ci, c0 in enumerate(range(0, d_ff, nc)):
        def conv(half, lo):
            buf = uc_ref.at[2 * (ci % 2) + half]
            u = _dot(h_ref[...], wup_ref[:, lo:lo + nc])
            buf[off - hist:off, :] = st_ref[:, lo:lo + nc]
            buf[off:off + tm, :] = u
            st_ref[:, lo:lo + nc] = u[tm - hist:, :]
            c = cb_ref[:, lo:lo + nc]
            for j in range(CONV_W - 1):
                r0 = off - (CONV_W - 1 - j) * nb
                c = c + cw_ref[j:j + 1, lo:lo + nc] * buf[r0:r0 + tm, :]
            return c + cw_ref[CONV_W - 1:CONV_W, lo:lo + nc] * u
        val = conv(0, c0)
        gate = conv(1, d_ff + c0)
        a_ref[:, c0:c0 + nc] = (jax.nn.silu(gate) * val).astype(BF16)

    o_ref[...] = x + _dot(a_ref[...], wdn_ref[...])


def _ffn_call(x, prev, g, w_up, conv_w, conv_b, w_down, *, nb, tm):
    nseq, rows, d = x.shape
    d_ff = w_down.shape[0]
    hist = (CONV_W - 1) * nb
    off = -(-hist // SUBLANES) * SUBLANES
    nc = FFN_CHUNK
    assert rows % tm == 0 and tm % nb == 0 and d_ff % nc == 0 and tm >= hist
    kern = functools.partial(_ffn_kernel, nb=nb, tm=tm, off=off, d_ff=d_ff, nc=nc)
    return pl.pallas_call(
        kern,
        grid=(nseq, rows // tm),
        in_specs=[
            pl.BlockSpec((None, tm, d), lambda s, i: (s, i, 0)),
            pl.BlockSpec((None, hist, 2 * d_ff), lambda s, i: (s, 0, 0)),
            _const_spec((1, d)),
            _const_spec((d, 2 * d_ff), single=True),
            _const_spec((CONV_W, 2 * d_ff)),
            _const_spec((1, 2 * d_ff)),
            _const_spec((d_ff, d), single=True),
        ],
        out_specs=[
            pl.BlockSpec((None, tm, d), lambda s, i: (s, i, 0)),
            pl.BlockSpec((None, hist, 2 * d_ff), lambda s, i: (s, 0, 0)),
        ],
        out_shape=[
            jax.ShapeDtypeStruct((nseq, rows, d), F32),
            jax.ShapeDtypeStruct((nseq, hist, 2 * d_ff), F32),
        ],
        scratch_shapes=[
            pltpu.VMEM((tm, d), BF16),
            pltpu.VMEM((4, off + tm, nc), F32),
            pltpu.VMEM((tm, d_ff), BF16),
        ],
        compiler_params=_params("arbitrary", "arbitrary"),
        name="conv_ffn",
    )(x, prev, g.reshape(1, d), w_up, conv_w, conv_b.reshape(1, 2 * d_ff), w_down)


def _rope128(x, c, sa, sb):
    n = x.shape[-1]
    return x * c + pltpu.roll(x, n - QK_ROPE // 2, 1) * sa + pltpu.roll(x, QK_ROPE // 2, 1) * sb


def _head_ms(x, ones2):
    sq = (x * x).astype(BF16)
    parts = [_dot(sq[:, c0:c0 + 2 * HEAD_PAD], ones2) for c0 in range(0, x.shape[-1], 2 * HEAD_PAD)]
    return jnp.concatenate(parts, axis=-1) * (1.0 / QK_HEAD)


def _mla_proj_kernel(x_ref, tc_ref, tsa_ref, tsb_ref, gin_ref, win_ref, gql_ref, wq_ref, gkvl_ref, wkv_ref,
                     gq_ref, gk_ref, ones2_ref,
                     q_ref, k_ref, v_ref, ckv_ref, kpe_ref):
    x = x_ref[...]
    h = _rms(x, gin_ref[...]).astype(BF16)
    proj = _dot(h, win_ref[...])
    cq = proj[:, :Q_LORA]
    ckv = proj[:, Q_LORA:Q_LORA + KV_LORA]
    kp = proj[:, Q_LORA + KV_LORA:]
    tc, tsa, tsb = tc_ref[...], tsa_ref[...], tsb_ref[...]
    ones2 = ones2_ref[...]

    q = _dot(_rms(cq, gql_ref[...]).astype(BF16), wq_ref[...])
    rep = lambda t: jnp.concatenate([t] * N_HEADS, axis=-1)
    q = _rope128(q, rep(tc), rep(tsa), rep(tsb))
    q = q * lax.rsqrt(_head_ms(q, ones2) + NORM_EPS) * gq_ref[...]
    q_ref[...] = q.astype(q_ref.dtype)

    ckv_n = _rms(ckv, gkvl_ref[...])
    ckv_ref[...] = ckv_n
    kp = _rope128(kp, tc, tsa, tsb)
    kpe_ref[...] = kp[:, QK_NOPE:QK_HEAD]

    kv = _dot(ckv_n.astype(BF16), wkv_ref[...])
    k = kv[:, :QK_PAD] + rep(kp)
    k = k * lax.rsqrt(_head_ms(k, ones2) + NORM_EPS) * gk_ref[...]
    k_ref[...] = k.astype(k_ref.dtype)
    v_ref[...] = kv[:, QK_PAD:].astype(v_ref.dtype)


def _mla_proj_call(x, tabs, w, *, tm):
    nseq, rows, d = x.shape
    assert rows % tm == 0
    row_spec = lambda n: pl.BlockSpec((None, tm, n), lambda s, i: (s, i, 0))
    tab_spec = pl.BlockSpec((tm, LANES), lambda s, i: (i, 0))
    consts = [w['g_in'], w['w_in'], w['g_ql'], w['w_q'], w['g_kvl'], w['w_kv'], w['g_q'], w['g_k'], w['ones2']]
    return pl.pallas_call(
        _mla_proj_kernel,
        grid=(nseq, rows // tm),
        in_specs=[row_spec(d), tab_spec, tab_spec, tab_spec] + [_const_spec(c.shape) for c in consts],
        out_specs=[row_spec(QK_PAD), row_spec(QK_PAD), row_spec(QK_PAD), row_spec(KV_LORA), row_spec(QK_ROPE)],
        out_shape=[
            jax.ShapeDtypeStruct((nseq, rows, QK_PAD), BF16),
            jax.ShapeDtypeStruct((nseq, rows, QK_PAD), BF16),
            jax.ShapeDtypeStruct((nseq, rows, QK_PAD), BF16),
            jax.ShapeDtypeStruct((nseq, rows, KV_LORA), F32),
            jax.ShapeDtypeStruct((nseq, rows, QK_ROPE), F32),
        ],
        compiler_params=_params("arbitrary", "arbitrary"),
        name="mla_proj",
    )(x, *tabs, *consts)


def _head_pad_cols(w, per_head, lo, hi):
    k = w.shape[0]
    w = w.reshape(k, N_HEADS, per_head)[:, :, lo:hi]
    w = jnp.pad(w, ((0, 0), (0, 0), (0, HEAD_PAD - (hi - lo))))
    return w.reshape(k, QK_PAD)


def _head_pad_vec(g):
    return jnp.tile(jnp.pad(g, (0, HEAD_PAD - g.shape[0])), N_HEADS).reshape(1, QK_PAD)


def _rope_tables(pos):
    half = QK_ROPE // 2
    inv = ROPE_THETA ** (-jnp.arange(half, dtype=F32) * (2.0 / QK_ROPE))
    ang = pos.astype(F32)[:, None] * inv[None, :]
    cos, sin = jnp.cos(ang), jnp.sin(ang)
    t = pos.shape[0]
    z = lambda n: jnp.zeros((t, n), F32)
    c = jnp.concatenate([jnp.ones((t, QK_NOPE), F32), cos, cos, z(HEAD_PAD - QK_HEAD)], axis=-1)
    sa = jnp.concatenate([z(QK_NOPE), -sin, z(half), z(HEAD_PAD - QK_HEAD)], axis=-1)
    sb = jnp.concatenate([z(QK_NOPE), z(half), sin, z(HEAD_PAD - QK_HEAD)], axis=-1)
    return c, sa, sb


def _mla_weights(g_in, w_in, g_ql, w_q_up, g_kvl, w_kv_up, g_qn, g_kn, w_out):
    d = w_in.shape[0]
    w_kpe = jnp.pad(w_in[:, Q_LORA + KV_LORA:], ((0, 0), (QK_NOPE, HEAD_PAD - QK_HEAD)))
    w_in_ext = jnp.concatenate([w_in[:, :Q_LORA + KV_LORA], w_kpe], axis=1)
    w_uk = _head_pad_cols(w_kv_up, QK_NOPE + V_HEAD, 0, QK_NOPE)
    w_uv = _head_pad_cols(w_kv_up, QK_NOPE + V_HEAD, QK_NOPE, QK_NOPE + V_HEAD)
    blk = np.kron(np.eye(2, dtype=np.float32), np.ones((HEAD_PAD, HEAD_PAD), np.float32))
    w_out_pad = jnp.pad(w_out.reshape(N_HEADS, V_HEAD, d), ((0, 0), (0, HEAD_PAD - V_HEAD), (0, 0)))
    return {
        'g_in': g_in.reshape(1, d),
        'w_in': w_in_ext.astype(BF16),
        'g_ql': g_ql.reshape(1, Q_LORA),
        'w_q': _head_pad_cols(w_q_up, QK_HEAD, 0, QK_HEAD).astype(BF16),
        'g_kvl': g_kvl.reshape(1, KV_LORA),
        'w_kv': jnp.concatenate([w_uk, w_uv], axis=1).astype(BF16),
        'g_q': _head_pad_vec(g_qn) * (QK_HEAD ** -0.5 * math.log2(math.e)),
        'g_k': _head_pad_vec(g_kn),
        'ones2': jnp.asarray(blk, BF16),
        'w_out': w_out_pad.reshape(QK_PAD, d).astype(BF16),
        'w_uk': w_uk, 'w_uv': w_uv, 'g_kn': g_kn,
    }


def _flash_kernel(it_ref, jt_ref, q_ref, k_ref, v_ref, x_ref, wo_ref, o_ref, m_ref, l_ref, acc_ref, *, tq, qs):
    tk = tq
    i = it_ref[pl.program_id(1)]
    j = jt_ref[pl.program_id(1)]

    @pl.when(j == 0)
    def _():
        m_ref[...] = jnp.full(m_ref.shape, -jnp.inf, F32)
        l_ref[...] = jnp.zeros(l_ref.shape, F32)
        acc_ref[...] = jnp.zeros(acc_ref.shape, F32)

    def step(diagonal):
        keep = {}
        if diagonal:
            for r0 in range(0, tq, qs):
                row = r0 + lax.broadcasted_iota(jnp.int32, (qs, r0 + qs), 0)
                keep[r0] = lax.broadcasted_iota(jnp.int32, (qs, r0 + qs), 1) <= row
        for h in range(N_HEADS):
            hs = slice(h * HEAD_PAD, (h + 1) * HEAD_PAD)
            for r0 in range(0, tq, qs):
                rs = slice(r0, r0 + qs)
                nk = r0 + qs if diagonal else tk
                s = _dot_nt(q_ref[rs, hs], k_ref[:nk, hs])
                if diagonal:
                    s = jnp.where(keep[r0], s, jnp.finfo(F32).min)
                m_old = m_ref[h, rs]
                m_new = jnp.maximum(m_old, jnp.max(s, axis=-1, keepdims=True))
                alpha = jnp.exp2(m_old - m_new)
                p = jnp.exp2(s - m_new[:, :1])
                l_ref[h, rs] = alpha * l_ref[h, rs] + jnp.sum(p, axis=-1, keepdims=True)
                acc_ref[h, rs] = alpha * acc_ref[h, rs] + _dot(p.astype(BF16), v_ref[:nk, hs])
                m_ref[h, rs] = m_new

    @pl.when(j < i)
    def _():
        step(False)

    @pl.when(j == i)
    def _():
        step(True)
        o = jnp.concatenate([acc_ref[h] / l_ref[h] for h in range(N_HEADS)], axis=-1)
        o_ref[...] = x_ref[...] + _dot(o.astype(BF16), wo_ref[...])


def _flash_call(q, k, v, x, w_out_pad, *, tq, qs):
    nseq, s, d = x.shape
    tk = tq
    assert s % tq == 0 and tq % qs == 0
    nq = s // tq
    pairs = [(i, j) for i in range(nq) for j in range(i + 1)]
    i_tab = jnp.asarray([p[0] for p in pairs], jnp.int32)
    j_tab = jnp.asarray([p[1] for p in pairs], jnp.int32)
    kern = functools.partial(_flash_kernel, tq=tq, qs=qs)
    q_map = lambda b, t, it, jt: (b, it[t], 0)
    kv_map = lambda b, t, it, jt: (b, jt[t], 0)
    return pl.pallas_call(
        kern,
        grid_spec=pltpu.PrefetchScalarGridSpec(
            num_scalar_prefetch=2,
            grid=(nseq, len(pairs)),
            in_specs=[
                pl.BlockSpec((None, tq, QK_PAD), q_map),
                pl.BlockSpec((None, tk, QK_PAD), kv_map),
                pl.BlockSpec((None, tk, QK_PAD), kv_map),
                pl.BlockSpec((None, tq, d), q_map),
                pl.BlockSpec((QK_PAD, d), lambda b, t, it, jt: (0, 0)),
            ],
            out_specs=pl.BlockSpec((None, tq, d), q_map),
            scratch_shapes=[
                pltpu.VMEM((N_HEADS, tq, HEAD_PAD), F32),
                pltpu.VMEM((N_HEADS, tq, HEAD_PAD), F32),
                pltpu.VMEM((N_HEADS, tq, HEAD_PAD), F32),
            ],
        ),
        out_shape=jax.ShapeDtypeStruct((nseq, s, d), F32),
        compiler_params=_params("arbitrary", "arbitrary"),
        name="mla_flash",
    )(i_tab, j_tab, q, k, v, x, w_out_pad)


NEW_PAD = LANES
KEY_TILE = 1024


def _paged_kernel(pt_ref, qlat_ref, qpe_ref, cnew_ref, rnew_ref, ckv_hbm, kpe_hbm, wukr_ref,
                  gkr_ref, o_ref, cbuf, rbuf, s_ref, cbf_ref, csem, rsem, *, layer, n_pages, n_new):
    b = pl.program_id(0)
    nb = pl.num_programs(0)
    slot = b % 2
    past = n_pages * PAGE_SIZE
    n_tiles = past // KEY_TILE
    ppt = KEY_TILE // PAGE_SIZE

    def start_tile(bb, sl, i):
        for k in range(ppt):
            p = i * ppt + k
            pg = pt_ref[bb, p]
            rows = pl.ds(pl.multiple_of(p * PAGE_SIZE, PAGE_SIZE), PAGE_SIZE)
            pltpu.make_async_copy(ckv_hbm.at[layer, pg], cbuf.at[sl, rows, :], csem.at[sl]).start()
            pltpu.make_async_copy(kpe_hbm.at[layer, pg], rbuf.at[sl, i, :, pl.ds(k * PAGE_SIZE, PAGE_SIZE)],
                                  rsem.at[sl]).start()

    @pl.when(b == 0)
    def _():
        def body(i, c):
            start_tile(0, 0, i)
            return c
        lax.fori_loop(0, n_tiles, body, 0)

    pltpu.make_async_copy(cbuf.at[slot], cbuf.at[slot], csem.at[slot]).wait()
    pltpu.make_async_copy(rbuf.at[slot], rbuf.at[slot], rsem.at[slot]).wait()

    nq = n_new * N_HEADS
    wcat = jnp.concatenate([wukr_ref[...], qlat_ref[...]], axis=0)
    qpe = (qpe_ref[...] * gkr_ref[...]).astype(BF16)
    n_kn = N_HEADS * QK_NOPE

    def scores(ckv_t, kpe_tt):
        nk = ckv_t.shape[0]
        cb = ckv_t.astype(BF16)
        a = _dot_nt(wcat, cb)
        kn = a[:n_kn]
        ssq = jnp.sum((kn * kn).reshape(QK_NOPE, N_HEADS, nk), axis=0)
        ssq = ssq + jnp.sum(kpe_tt * kpe_tt, axis=0, keepdims=True)
        r = lax.rsqrt(ssq * (1.0 / QK_HEAD) + NORM_EPS)
        s = a[n_kn:] + _dot(qpe, kpe_tt.astype(BF16))
        return s * jnp.concatenate([r] * n_new, axis=0), cb

    nxt = jnp.minimum(b + 1, nb - 1)

    def tile_rows(i):
        return pl.ds(pl.multiple_of(i * KEY_TILE, KEY_TILE), KEY_TILE)

    def pass1(i, m, per_iter):
        for k in range(per_iter):
            start_tile(nxt, 1 - slot, per_iter * i + k)
        rows = tile_rows(i)
        s, cb = scores(cbuf[slot, rows, :], rbuf[slot, i])
        s_ref[i] = s
        cbf_ref[rows, :] = cb
        return jnp.maximum(m, jnp.max(s, axis=1, keepdims=True))

    per_iter = 2 if n_tiles % 2 == 0 else 1
    n_issue = n_tiles // per_iter
    m = lax.fori_loop(0, n_issue, functools.partial(pass1, per_iter=per_iter),
                      jnp.full((nq, 1), -jnp.inf, F32), unroll=2)
    m = lax.fori_loop(n_issue, n_tiles, functools.partial(pass1, per_iter=0), m, unroll=2)

    @pl.when(b == nb - 1)
    def _():
        pltpu.make_async_copy(cbuf.at[1 - slot], cbuf.at[1 - slot], csem.at[1 - slot]).wait()
        pltpu.make_async_copy(rbuf.at[1 - slot], rbuf.at[1 - slot], rsem.at[1 - slot]).wait()

    s_new, cb_new = scores(cnew_ref[...], rnew_ref[...])
    key = lax.broadcasted_iota(jnp.int32, (nq, NEW_PAD), 1)
    qt = lax.broadcasted_iota(jnp.int32, (nq, NEW_PAD), 0) // N_HEADS
    s_new = jnp.where((key < n_new) & (key <= qt), s_new, jnp.finfo(F32).min)
    m = jnp.maximum(m, jnp.max(s_new, axis=1, keepdims=True))

    p_new = jnp.exp2(s_new - m)
    l0 = jnp.sum(p_new, axis=1, keepdims=True)
    acc0 = _dot(p_new.astype(BF16), cb_new)

    def pass2(i, carry):
        l, acc = carry
        p = jnp.exp2(s_ref[i] - m)
        return l + jnp.sum(p, axis=1, keepdims=True), acc + _dot(p.astype(BF16), cbf_ref[tile_rows(i), :])

    l, o_lat = lax.fori_loop(0, n_tiles, pass2, (l0, acc0), unroll=True)

    o_ref[...] = o_lat / l


def _paged_call(page_table, qlat, qpe, ckv_new, kpe_new, cache_ckv, cache_kpe, w, *, layer, n_new):
    nb, nq, _ = qlat.shape
    n_pages = page_table.shape[1]
    past = n_pages * PAGE_SIZE
    assert past % KEY_TILE == 0 and n_new <= NEW_PAD and nq == n_new * N_HEADS
    kern = functools.partial(_paged_kernel, layer=layer, n_pages=n_pages, n_new=n_new)
    consts = [w['w_uk_r'], w['g_kr']]
    seq = lambda n1, n2: pl.BlockSpec((None, n1, n2), lambda b, pt: (b, 0, 0))
    cspec = lambda c: pl.BlockSpec(c.shape, lambda b, pt: (0,) * c.ndim)
    return pl.pallas_call(
        kern,
        grid_spec=pltpu.PrefetchScalarGridSpec(
            num_scalar_prefetch=1,
            grid=(nb,),
            in_specs=[seq(nq, KV_LORA), seq(nq, QK_ROPE), seq(NEW_PAD, KV_LORA), seq(QK_ROPE, NEW_PAD),
                      pl.BlockSpec(memory_space=pl.ANY), pl.BlockSpec(memory_space=pl.ANY)]
                     + [cspec(c) for c in consts],
            out_specs=seq(nq, KV_LORA),
            scratch_shapes=[
                pltpu.VMEM((2, past, KV_LORA), F32),
                pltpu.VMEM((2, past // KEY_TILE, QK_ROPE, KEY_TILE), F32),
                pltpu.VMEM((past // KEY_TILE, nq, KEY_TILE), F32),
                pltpu.VMEM((past, KV_LORA), BF16),
                pltpu.SemaphoreType.DMA((2,)),
                pltpu.SemaphoreType.DMA((2,)),
            ],
        ),
        out_shape=jax.ShapeDtypeStruct((nb, nq, KV_LORA), F32),
        compiler_params=_params("arbitrary"),
        name="mla_paged",
    )(page_table, qlat, qpe, ckv_new, kpe_new, cache_ckv, cache_kpe, *consts)


def _paged_weights(w):
    g_kn = w['g_kn']
    g_pad = jnp.tile(jnp.pad(g_kn[:QK_NOPE], (0, HEAD_PAD - QK_NOPE)), N_HEADS)
    w_uk_r = w['w_uk'].reshape(KV_LORA, N_HEADS, HEAD_PAD)[:, :, :QK_NOPE].transpose(2, 1, 0)
    return {
        'w_ukt': (w['w_uk'].T * g_pad[:, None]).astype(BF16),
        'w_uk_r': w_uk_r.reshape(QK_NOPE * N_HEADS, KV_LORA).astype(BF16),
        'g_kr': g_kn[QK_NOPE:].reshape(1, QK_ROPE),
        'w_uv_p': w['w_uv'].astype(BF16),
    }


def _absorb_kernel(q_ref, w_ref, o_ref):
    for h in range(N_HEADS):
        hs = slice(h * HEAD_PAD, (h + 1) * HEAD_PAD)
        o_ref[h] = _dot(q_ref[:, hs], w_ref[hs, :]).astype(o_ref.dtype)


def _absorb_call(q, w_ukt):
    rows = q.shape[0]
    return pl.pallas_call(
        _absorb_kernel,
        out_shape=jax.ShapeDtypeStruct((N_HEADS, rows, KV_LORA), BF16),
        compiler_params=pltpu.CompilerParams(vmem_limit_bytes=VMEM_LIMIT_BYTES),
        name="mla_absorb",
    )(q, w_ukt)


def _out_proj_kernel(x_ref, ol_ref, wuv_ref, wo_ref, y_ref):
    o = jnp.concatenate([_dot(ol_ref[h].astype(BF16), wuv_ref[:, h * HEAD_PAD:(h + 1) * HEAD_PAD])
                         for h in range(N_HEADS)], axis=-1)
    y_ref[...] = x_ref[...] + _dot(o.astype(BF16), wo_ref[...])


def _out_proj_call(x, o_lat, w_uv_pad, w_out_pad):
    return pl.pallas_call(
        _out_proj_kernel,
        out_shape=jax.ShapeDtypeStruct(x.shape, F32),
        compiler_params=pltpu.CompilerParams(vmem_limit_bytes=VMEM_LIMIT_BYTES),
        name="mla_out_proj",
    )(x, o_lat, w_uv_pad, w_out_pad)


def _s5_lambda(ldt, ar, ai):
    dt = jnp.exp(ldt)
    mag = jnp.exp(ar * dt)
    lr = mag * jnp.cos(ai * dt)
    li = mag * jnp.sin(ai * dt)
    den = ar * ar + ai * ai
    cr = ((lr - 1.0) * ar + li * ai) / den
    ci = (li * ar - (lr - 1.0) * ai) / den
    return lr, li, cr, ci


def _s5_disc_kernel(ldt_ref, ar_ref, ai_ref, ldte_ref, are_ref, aie_ref, br_ref, bi_ref,
                    lr_ref, li_ref, bbr_ref, bbi_ref):
    lr, li, _, _ = _s5_lambda(ldt_ref[...], ar_ref[...], ai_ref[...])
    lr_ref[...] = lr
    li_ref[...] = li
    _, _, cr, ci = _s5_lambda(ldte_ref[...], are_ref[...], aie_ref[...])
    br, bi = br_ref[...], bi_ref[...]
    bbr_ref[...] = cr * br - ci * bi
    bbi_ref[...] = cr * bi + ci * br


def _s5_disc_call(log_dt, a_re, a_im, b_re, b_im):
    g, p, c = b_re.shape
    flat = lambda a: jnp.broadcast_to(a[:, :, None], (g, p, c)).reshape(-1, LANES)
    outs = pl.pallas_call(
        _s5_disc_kernel,
        out_shape=[jax.ShapeDtypeStruct((g, p), F32)] * 2 + [jax.ShapeDtypeStruct((g * p * c // LANES, LANES), F32)] * 2,
        name="s5_discretise",
    )(log_dt, a_re, a_im, flat(log_dt), flat(a_re), flat(a_im), b_re.reshape(-1, LANES), b_im.reshape(-1, LANES))
    lr, li, bbr, bbi = outs
    return lr, li, bbr.reshape(g, p, c), bbi.reshape(g, p, c)


S5_GB = 8
S5_CW = S5_GB * SSM_GROUP
S5_SW = S5_GB * SSM_STATE
S5_LC = 1024


def _s5_kernel(x_ref, s0_ref, perm_ref, permt_ref, g_ref, win_ref, wb_ref, lam_ref, wc_ref, dsk_ref, wout_ref,
               o_ref, st_ref, u_ref, xs_ref, y_ref, *, nb, nt, nsub):
    i = pl.program_id(0)
    d = x_ref.shape[-1]
    ns = st_ref.shape[-1] // 2
    nts = nt // nsub
    rs = nts * nb

    @pl.when(i == 0)
    def _():
        st_ref[...] = s0_ref[...]

    x = x_ref[...].reshape(nb * nt, d)
    h = _rms(x, g_ref[...]).astype(BF16)
    h = _dot(perm_ref[...], h).astype(BF16)
    u_ref[...] = _dot(h, win_ref[...])

    for sub in range(nsub):
        rows = slice(sub * rs, (sub + 1) * rs)
        xs = xs_ref.at[sub % 2]
        for j in range(d // S5_CW):
            xb = _dot(u_ref[rows, j * S5_CW:(j + 1) * S5_CW].astype(BF16), wb_ref[j])
            xs[:, j * S5_SW:(j + 1) * S5_SW] = xb[:, :S5_SW]
            xs[:, ns + j * S5_SW:ns + (j + 1) * S5_SW] = xb[:, S5_SW:]

        def scan_rows(b0):
            for lc in range(0, ns, S5_LC):
                re, im = slice(lc, lc + S5_LC), slice(ns + lc, ns + lc + S5_LC)
                sr, si = st_ref[pl.ds(b0, SUBLANES), re], st_ref[pl.ds(b0, SUBLANES), im]
                for t in range(nts):
                    r8 = pl.ds(t * nb + b0, SUBLANES)
                    lr, li = lam_ref[:, re], lam_ref[:, im]
                    sr, si = lr * sr - li * si + xs[r8, re], lr * si + li * sr + xs[r8, im]
                    xs[r8, re] = sr
                    xs[r8, im] = si
                st_ref[pl.ds(b0, SUBLANES), re] = sr
                st_ref[pl.ds(b0, SUBLANES), im] = si

        if nb == SUBLANES:
            scan_rows(0)
        else:
            def scan_tile(bt, carry):
                scan_rows(pl.multiple_of(bt * SUBLANES, SUBLANES))
                return carry
            lax.fori_loop(0, nb // SUBLANES, scan_tile, 0)

        for j in range(d // S5_CW):
            cs = slice(j * S5_CW, (j + 1) * S5_CW)
            s_cat = jnp.concatenate([xs[:, j * S5_SW:(j + 1) * S5_SW],
                                     xs[:, ns + j * S5_SW:ns + (j + 1) * S5_SW]], axis=-1)
            y = _dot(s_cat.astype(BF16), wc_ref[j]) + dsk_ref[:, cs] * u_ref[rows, cs]
            y_ref[rows, cs] = jax.nn.gelu(y).astype(BF16)

    gl = _dot(_dot(permt_ref[...], y_ref[...]).astype(BF16), wout_ref[...])
    o_ref[...] = (x + gl[:, :d] * jax.nn.sigmoid(gl[:, d:])).reshape(o_ref.shape)


def _s5_call(x3, s0, w, *, nb, nt, nsub, block, n_blocks):
    d = x3.shape[-1]
    r = nb * nt
    ns2 = s0.shape[-1]
    assert block[0] * block[1] == r and nt % nsub == 0 and nb % SUBLANES == 0 and d % S5_CW == 0
    assert block[1] % SUBLANES == 0
    perm = np.zeros((r, r), np.float32)
    bb, tt = np.meshgrid(np.arange(nb), np.arange(nt), indexing='ij')
    perm[(tt * nb + bb).ravel(), (bb * nt + tt).ravel()] = 1.0
    kern = functools.partial(_s5_kernel, nb=nb, nt=nt, nsub=nsub)
    consts = [s0, jnp.asarray(perm, BF16), jnp.asarray(perm.T, BF16),
              w['g'], w['w_in'], w['wb'], w['lam'], w['wc'], w['dsk'], w['w_out']]
    x_spec = pl.BlockSpec(tuple(block) + (d,), lambda i: (0, i, 0))
    return pl.pallas_call(
        kern,
        grid=(n_blocks,),
        in_specs=[x_spec] + [_const_spec(c.shape) for c in consts],
        out_specs=[x_spec, _const_spec((nb, ns2))],
        out_shape=[jax.ShapeDtypeStruct(x3.shape, F32), jax.ShapeDtypeStruct((nb, ns2), F32)],
        scratch_shapes=[
            pltpu.VMEM((r, d), F32),
            pltpu.VMEM((2, r // nsub, ns2), F32),
            pltpu.VMEM((r, d), BF16),
        ],
        compiler_params=_params("arbitrary"),
        name="s5_mixer",
    )(x3, *consts)


def _s5_weights(g, w_in, log_dt, a_re, a_im, b_re, b_im, c_re, c_im, d_skip, w_out):
    d = w_in.shape[0]
    ng, p, c = b_re.shape
    lr, li, bbr, bbi = _s5_disc_call(log_dt, a_re, a_im, b_re, b_im)
    eye = jnp.eye(S5_GB, dtype=F32)
    nblk = ng // S5_GB

    def b_tiles(bb):
        t = bb.reshape(nblk, S5_GB, p, c).transpose(0, 1, 3, 2)
        return jnp.einsum('ab,jacp->jacbp', eye, t).reshape(nblk, S5_CW, S5_SW)

    def c_tiles(cc):
        t = cc.reshape(nblk, S5_GB, c, p)
        return jnp.einsum('ba,jacp->jbpac', eye, t).reshape(nblk, S5_SW, S5_CW)

    lam = jnp.concatenate([lr.reshape(1, -1), li.reshape(1, -1)], axis=-1)
    return {
        'g': g.reshape(1, d),
        'w_in': w_in.astype(BF16),
        'wb': jnp.concatenate([b_tiles(bbr), b_tiles(bbi)], axis=-1).astype(BF16),
        'lam': jnp.broadcast_to(lam, (SUBLANES, lam.shape[-1])),
        'wc': jnp.concatenate([c_tiles(c_re), -c_tiles(c_im)], axis=1).astype(BF16),
        'dsk': d_skip.reshape(1, d),
        'w_out': w_out.astype(BF16),
    }


PROMPT_TM = 512
FLASH_TQ = 512
FLASH_QS = 512
FFN_TM = 1024
S5_ROWS = 512
S5_NSUB = 2


def _mla_prompt_layer(x, w):
    b, s, d = x.shape
    tabs = _rope_tables(jnp.arange(s))
    q, k, v, ckv, kpe = _mla_proj_call(x, tabs, w, tm=min(PROMPT_TM, s))
    y = _flash_call(q, k, v, x, w['w_out'], tq=min(FLASH_TQ, s), qs=min(FLASH_QS, s))
    return y, ckv, kpe


def _mla_sample_layer(x, page_table, cache_ckv, cache_kpe, w, *, layer):
    b, t, d = x.shape
    past = page_table.shape[1] * PAGE_SIZE
    tabs = [jnp.tile(tb, (b, 1)) for tb in _rope_tables(past + jnp.arange(t))]
    q, _, _, ckv, kpe = _mla_proj_call(x.reshape(1, b * t, d), tabs, w, tm=b * t)
    ckv, kpe = ckv.reshape(b, t, KV_LORA), kpe.reshape(b, t, QK_ROPE)
    pad = lambda a: jnp.pad(a, ((0, 0), (0, NEW_PAD - t), (0, 0)))
    pw = _paged_weights(w)
    q = q.reshape(b * t, QK_PAD)
    nq = t * N_HEADS
    qlat = jnp.swapaxes(_absorb_call(q, pw['w_ukt']), 0, 1).reshape(b, nq, KV_LORA)
    qpe = q.reshape(b, t, N_HEADS, HEAD_PAD)[..., QK_NOPE:QK_HEAD].astype(F32).reshape(b, nq, QK_ROPE)
    o_lat = _paged_call(page_table, qlat, qpe, pad(ckv), jnp.swapaxes(pad(kpe), 1, 2),
                        cache_ckv, jnp.swapaxes(cache_kpe, 2, 3), pw, layer=layer, n_new=t)
    o_lat = jnp.moveaxis(o_lat.reshape(b * t, N_HEADS, KV_LORA), 1, 0)
    y = _out_proj_call(x.reshape(b * t, d), o_lat, pw['w_uv_p'], w['w_out'])
    return y.reshape(b, t, d), ckv, kpe


def _s5_layer(x, s0_re, s0_im, w):
    b, t, d = x.shape
    nt = min(S5_ROWS // b, t)
    s0 = jnp.concatenate([s0_re.reshape(b, -1), s0_im.reshape(b, -1)], axis=-1)
    if nt % SUBLANES == 0:
        y, st = _s5_call(x, s0, w, nb=b, nt=nt, nsub=S5_NSUB, block=(b, nt), n_blocks=t // nt)
    else:
        assert nt == t
        x3 = x.reshape(b * t // SUBLANES, SUBLANES, d)
        y, st = _s5_call(x3, s0, w, nb=b, nt=nt, nsub=S5_NSUB, block=x3.shape[:2], n_blocks=1)
        y = y.reshape(b, t, d)
    ns = st.shape[-1] // 2
    return y, st[:, :ns].reshape(s0_re.shape), st[:, ns:].reshape(s0_im.shape)


def _ffn_prompt_layer(x, w):
    b, s, d = x.shape
    zero = jnp.zeros((b, CONV_W - 1, w[1].shape[-1]), F32)
    return _ffn_call(x, zero, *w, nb=1, tm=min(FFN_TM, s))


def _ffn_sample_layer(x, prev, w):
    b, t, d = x.shape
    x_tm = jnp.swapaxes(x, 0, 1).reshape(1, t * b, d)
    prev_tm = jnp.swapaxes(prev, 0, 1).reshape(1, (CONV_W - 1) * b, prev.shape[-1])
    y, st = _ffn_call(x_tm, prev_tm, *w, nb=b, tm=t * b)
    return jnp.swapaxes(y.reshape(t, b, d), 0, 1), jnp.swapaxes(st.reshape(CONV_W - 1, b, -1), 0, 1)


def kernel(x_prompt, x_sample, cache_ckv, cache_kpe, state_ssm_re, state_ssm_im, state_conv, page_table,
           attn_norm, attn_w_in, attn_q_lat_norm, attn_w_q_up, attn_kv_lat_norm, attn_w_kv_up,
           attn_q_norm, attn_k_norm, attn_w_out,
           ssm_norm, ssm_w_in, ssm_log_dt, ssm_a_re, ssm_a_im, ssm_b_re, ssm_b_im,
           ssm_c_re, ssm_c_im, ssm_d, ssm_w_out,
           ffn_norm, ffn_w_up, ffn_conv_w, ffn_conv_b, ffn_w_down):
    xp, xs = x_prompt, x_sample
    depth = ffn_w_up.shape[0]
    bp = xp.shape[0]
    ssm_zero = jnp.zeros((bp,) + state_ssm_re.shape[2:], F32)
    ckv_p, kpe_p, ckv_s, kpe_s = [], [], [], []
    sre_p, sim_p, sre_s, sim_s = [], [], [], []
    conv_p, conv_s = [], []
    for i in range(depth):
        j = i // 2
        if i % 2 == 0:
            w = _mla_weights(attn_norm[j], attn_w_in[j], attn_q_lat_norm[j], attn_w_q_up[j], attn_kv_lat_norm[j],
                             attn_w_kv_up[j], attn_q_norm[j], attn_k_norm[j], attn_w_out[j])
            xp, c_p, r_p = _mla_prompt_layer(xp, w)
            xs, c_s, r_s = _mla_sample_layer(xs, page_table, cache_ckv, cache_kpe, w, layer=j)
            ckv_p.append(c_p)
            kpe_p.append(r_p)
            ckv_s.append(c_s)
            kpe_s.append(r_s)
        else:
            w = _s5_weights(ssm_norm[j], ssm_w_in[j], ssm_log_dt[j], ssm_a_re[j], ssm_a_im[j], ssm_b_re[j],
                            ssm_b_im[j], ssm_c_re[j], ssm_c_im[j], ssm_d[j], ssm_w_out[j])
            xp, r_p, i_p = _s5_layer(xp, ssm_zero, ssm_zero, w)
            xs, r_s, i_s = _s5_layer(xs, state_ssm_re[j], state_ssm_im[j], w)
            sre_p.append(r_p)
            sim_p.append(i_p)
            sre_s.append(r_s)
            sim_s.append(i_s)
        fw = (ffn_norm[i], ffn_w_up[i].astype(BF16), ffn_conv_w[i], ffn_conv_b[i], ffn_w_down[i].astype(BF16))
        xp, cv_p = _ffn_prompt_layer(xp, fw)
        xs, cv_s = _ffn_sample_layer(xs, state_conv[i], fw)
        conv_p.append(cv_p)
        conv_s.append(cv_s)
    return (xp, xs,
            jnp.stack(ckv_p), jnp.stack(kpe_p), jnp.stack(sre_p), jnp.stack(sim_p), jnp.stack(conv_p),
            jnp.stack(ckv_s), jnp.stack(kpe_s), jnp.stack(sre_s), jnp.stack(sim_s), jnp.stack(conv_s))
```

```python
import functools
import math

import jax
import jax.numpy as jnp
import numpy as np
from jax import lax
from jax.experimental import pallas as pl
from jax.experimental.pallas import tpu as pltpu

LANES = 128
SUBLANES = 8
VMEM_LIMIT_BYTES = 56 * 1024 * 1024

N_HEADS = 8
QK_NOPE = 64
QK_ROPE = 32
QK_HEAD = QK_NOPE + QK_ROPE
V_HEAD = 64
HEAD_PAD = LANES
QK_PAD = N_HEADS * HEAD_PAD
Q_LORA = 384
KV_LORA = 256
PAGE_SIZE = 128
ROPE_THETA = 10000.0
SSM_GROUP = 16
SSM_STATE = 64
CONV_W = 3
NORM_EPS = 1e-6

BF16 = jnp.bfloat16
F32 = jnp.float32


def _params(*sem):
    return pltpu.CompilerParams(dimension_semantics=sem, vmem_limit_bytes=VMEM_LIMIT_BYTES)


def _dot(a, b):
    return jnp.dot(a, b, preferred_element_type=F32)


def _dot_nt(a, b):
    return lax.dot_general(a, b, (((1,), (1,)), ((), ())), preferred_element_type=F32)


def _dot_tn(a, b):
    return lax.dot_general(a, b, (((0,), (0,)), ((), ())), preferred_element_type=F32)


def _rms(x, g):
    return x * lax.rsqrt(jnp.mean(x * x, axis=-1, keepdims=True) + NORM_EPS) * g


def _const_spec(shape, single=False):
    n = len(shape)
    if single:
        return pl.BlockSpec(shape, lambda *_: (0,) * n, pipeline_mode=pl.Buffered(1))
    return pl.BlockSpec(shape, lambda *_: (0,) * n)


FFN_CHUNK = 256

def _ffn_kernel(x_ref, prev_ref, g_ref, wup_ref, cw_ref, cb_ref, wdn_ref, o_ref, st_ref, h_ref, uc_ref, a_ref,
                *, nb, tm, off, d_ff, nc):
    hist = (CONV_W - 1) * nb

    @pl.when(pl.program_id(1) == 0)
    def _():
        st_ref[...] = prev_ref[...]

    x = x_ref[...]
    h_ref[...] = _rms(x, g_ref[...]).astype(BF16)

    for ci, c0 in enumerate(range(0, d_ff, nc)):
        def conv(half, lo):
            buf = uc_ref.at[2 * (ci % 2) + half]
            u = _dot(h_ref[...], wup_ref[:, lo:lo + nc])
            buf[off - hist:off, :] = st_ref[:, lo:lo + nc]
            buf[off:off + tm, :] = u
            st_ref[:, lo:lo + nc] = u[tm - hist:, :]
            c = cb_ref[:, lo:lo + nc]
            for j in range(CONV_W - 1):
                r0 = off - (CONV_W - 1 - j) * nb
                c = c + cw_ref[j:j + 1, lo:lo + nc] * buf[r0:r0 + tm, :]
            return c + cw_ref[CONV_W - 1:CONV_W, lo:lo + nc] * u
        val = conv(0, c0)
        gate = conv(1, d_ff + c0)
        a_ref[:, c0:c0 + nc] = (jax.nn.silu(gate) * val).astype(BF16)

    o_ref[...] = x + _dot(a_ref[...], wdn_ref[...])


def _ffn_call(x, prev, g, w_up, conv_w, conv_b, w_down, *, nb, tm):
    nseq, rows, d = x.shape
    d_ff = w_down.shape[0]
    hist = (CONV_W - 1) * nb
    off = -(-hist // SUBLANES) * SUBLANES
    nc = FFN_CHUNK
    assert rows % tm == 0 and tm % nb == 0 and d_ff % nc == 0 and tm >= hist
    kern = functools.partial(_ffn_kernel, nb=nb, tm=tm, off=off, d_ff=d_ff, nc=nc)
    return pl.pallas_call(
        kern,
        grid=(nseq, rows // tm),
        in_specs=[
            pl.BlockSpec((None, tm, d), lambda s, i: (s, i, 0)),
            pl.BlockSpec((None, hist, 2 * d_ff), lambda s, i: (s, 0, 0)),
            _const_spec((1, d)),
            _const_spec((d, 2 * d_ff), single=True),
            _const_spec((CONV_W, 2 * d_ff)),
            _const_spec((1, 2 * d_ff)),
            _const_spec((d_ff, d), single=True),
        ],
        out_specs=[
            pl.BlockSpec((None, tm, d), lambda s, i: (s, i, 0)),
            pl.BlockSpec((None, hist, 2 * d_ff), lambda s, i: (s, 0, 0)),
        ],
        out_shape=[
            jax.ShapeDtypeStruct((nseq, rows, d), F32),
            jax.ShapeDtypeStruct((nseq, hist, 2 * d_ff), F32),
        ],
        scratch_shapes=[
            pltpu.VMEM((tm, d), BF16),
            pltpu.VMEM((4, off + tm, nc), F32),
            pltpu.VMEM((tm, d_ff), BF16),
        ],
        compiler_params=_params("arbitrary", "arbitrary"),
        name="conv_ffn",
    )(x, prev, g.reshape(1, d), w_up, conv_w, conv_b.reshape(1, 2 * d_ff), w_down)


def _rope128(x, c, sa, sb):
    n = x.shape[-1]
    return x * c + pltpu.roll(x, n - QK_ROPE // 2, 1) * sa + pltpu.roll(x, QK_ROPE // 2, 1) * sb


def _head_ms(x, ones2):
    sq = (x * x).astype(BF16)
    parts = [_dot(sq[:, c0:c0 + 2 * HEAD_PAD], ones2) for c0 in range(0, x.shape[-1], 2 * HEAD_PAD)]
    return jnp.concatenate(parts, axis=-1) * (1.0 / QK_HEAD)


def _mla_proj_kernel(x_ref, tc_ref, tsa_ref, tsb_ref, gin_ref, win_ref, gql_ref, wq_ref, gkvl_ref, wkv_ref,
                     gq_ref, gk_ref, ones2_ref,
                     q_ref, k_ref, v_ref, ckv_ref, kpe_ref):
    x = x_ref[...]
    h = _rms(x, gin_ref[...]).astype(BF16)
    proj = _dot(h, win_ref[...])
    cq = proj[:, :Q_LORA]
    ckv = proj[:, Q_LORA:Q_LORA + KV_LORA]
    kp = proj[:, Q_LORA + KV_LORA:]
    tc, tsa, tsb = tc_ref[...], tsa_ref[...], tsb_ref[...]
    ones2 = ones2_ref[...]

    q = _dot(_rms(cq, gql_ref[...]).astype(BF16), wq_ref[...])
    rep = lambda t: jnp.concatenate([t] * N_HEADS, axis=-1)
    q = _rope128(q, rep(tc), rep(tsa), rep(tsb))
    q = q * lax.rsqrt(_head_ms(q, ones2) + NORM_EPS) * gq_ref[...]
    q_ref[...] = q.astype(q_ref.dtype)

    ckv_n = _rms(ckv, gkvl_ref[...])
    ckv_ref[...] = ckv_n
    kp = _rope128(kp, tc, tsa, tsb)
    kpe_ref[...] = kp[:, QK_NOPE:QK_HEAD]

    kv = _dot(ckv_n.astype(BF16), wkv_ref[...])
    k = kv[:, :QK_PAD] + rep(kp)
    k = k * lax.rsqrt(_head_ms(k, ones2) + NORM_EPS) * gk_ref[...]
    k_ref[...] = k.astype(k_ref.dtype)
    v_ref[...] = kv[:, QK_PAD:].astype(v_ref.dtype)


def _mla_proj_call(x, tabs, w, *, tm):
    nseq, rows, d = x.shape
    assert rows % tm == 0
    row_spec = lambda n: pl.BlockSpec((None, tm, n), lambda s, i: (s, i, 0))
    tab_spec = pl.BlockSpec((tm, LANES), lambda s, i: (i, 0))
    consts = [w['g_in'], w['w_in'], w['g_ql'], w['w_q'], w['g_kvl'], w['w_kv'], w['g_q'], w['g_k'], w['ones2']]
    return pl.pallas_call(
        _mla_proj_kernel,
        grid=(nseq, rows // tm),
        in_specs=[row_spec(d), tab_spec, tab_spec, tab_spec] + [_const_spec(c.shape) for c in consts],
        out_specs=[row_spec(QK_PAD), row_spec(QK_PAD), row_spec(QK_PAD), row_spec(KV_LORA), row_spec(QK_ROPE)],
        out_shape=[
            jax.ShapeDtypeStruct((nseq, rows, QK_PAD), BF16),
            jax.ShapeDtypeStruct((nseq, rows, QK_PAD), BF16),
            jax.ShapeDtypeStruct((nseq, rows, QK_PAD), BF16),
            jax.ShapeDtypeStruct((nseq, rows, KV_LORA), F32),
            jax.ShapeDtypeStruct((nseq, rows, QK_ROPE), F32),
        ],
        compiler_params=_params("arbitrary", "arbitrary"),
        name="mla_proj",
    )(x, *tabs, *consts)


def _head_pad_cols(w, per_head, lo, hi):
    k = w.shape[0]
    w = w.reshape(k, N_HEADS, per_head)[:, :, lo:hi]
    w = jnp.pad(w, ((0, 0), (0, 0), (0, HEAD_PAD - (hi - lo))))
    return w.reshape(k, QK_PAD)


def _head_pad_vec(g):
    return jnp.tile(jnp.pad(g, (0, HEAD_PAD - g.shape[0])), N_HEADS).reshape(1, QK_PAD)


def _rope_tables(pos):
    half = QK_ROPE // 2
    inv = ROPE_THETA ** (-jnp.arange(half, dtype=F32) * (2.0 / QK_ROPE))
    ang = pos.astype(F32)[:, None] * inv[None, :]
    cos, sin = jnp.cos(ang), jnp.sin(ang)
    t = pos.shape[0]
    z = lambda n: jnp.zeros((t, n), F32)
    c = jnp.concatenate([jnp.ones((t, QK_NOPE), F32), cos, cos, z(HEAD_PAD - QK_HEAD)], axis=-1)
    sa = jnp.concatenate([z(QK_NOPE), -sin, z(half), z(HEAD_PAD - QK_HEAD)], axis=-1)
    sb = jnp.concatenate([z(QK_NOPE), z(half), sin, z(HEAD_PAD - QK_HEAD)], axis=-1)
    return c, sa, sb


def _mla_weights(g_in, w_in, g_ql, w_q_up, g_kvl, w_kv_up, g_qn, g_kn, w_out):
    d = w_in.shape[0]
    w_kpe = jnp.pad(w_in[:, Q_LORA + KV_LORA:], ((0, 0), (QK_NOPE, HEAD_PAD - QK_HEAD)))
    w_in_ext = jnp.concatenate([w_in[:, :Q_LORA + KV_LORA], w_kpe], axis=1)
    w_uk = _head_pad_cols(w_kv_up, QK_NOPE + V_HEAD, 0, QK_NOPE)
    w_uv = _head_pad_cols(w_kv_up, QK_NOPE + V_HEAD, QK_NOPE, QK_NOPE + V_HEAD)
    blk = np.kron(np.eye(2, dtype=np.float32), np.ones((HEAD_PAD, HEAD_PAD), np.float32))
    w_out_pad = jnp.pad(w_out.reshape(N_HEADS, V_HEAD, d), ((0, 0), (0, HEAD_PAD - V_HEAD), (0, 0)))
    return {
        'g_in': g_in.reshape(1, d),
        'w_in': w_in_ext.astype(BF16),
        'g_ql': g_ql.reshape(1, Q_LORA),
        'w_q': _head_pad_cols(w_q_up, QK_HEAD, 0, QK_HEAD).astype(BF16),
        'g_kvl': g_kvl.reshape(1, KV_LORA),
        'w_kv': jnp.concatenate([w_uk, w_uv], axis=1).astype(BF16),
        'g_q': _head_pad_vec(g_qn) * (QK_HEAD ** -0.5 * math.log2(math.e)),
        'g_k': _head_pad_vec(g_kn),
        'ones2': jnp.asarray(blk, BF16),
        'w_out': w_out_pad.reshape(QK_PAD, d).astype(BF16),
        'w_uk': w_uk, 'w_uv': w_uv, 'g_kn': g_kn,
    }


def _flash_kernel(it_ref, jt_ref, q_ref, k_ref, v_ref, x_ref, wo_ref, o_ref, m_ref, l_ref, acc_ref, *, tq, qs):
    tk = tq
    i = it_ref[pl.program_id(1)]
    j = jt_ref[pl.program_id(1)]

    @pl.when(j == 0)
    def _():
        m_ref[...] = jnp.full(m_ref.shape, -jnp.inf, F32)
        l_ref[...] = jnp.zeros(l_ref.shape, F32)
        acc_ref[...] = jnp.zeros(acc_ref.shape, F32)

    def step(diagonal):
        keep = {}
        if diagonal:
            for r0 in range(0, tq, qs):
                row = r0 + lax.broadcasted_iota(jnp.int32, (qs, r0 + qs), 0)
                keep[r0] = lax.broadcasted_iota(jnp.int32, (qs, r0 + qs), 1) <= row
        for h in range(N_HEADS):
            hs = slice(h * HEAD_PAD, (h + 1) * HEAD_PAD)
            for r0 in range(0, tq, qs):
                rs = slice(r0, r0 + qs)
                nk = r0 + qs if diagonal else tk
                s = _dot_nt(q_ref[rs, hs], k_ref[:nk, hs])
                if diagonal:
                    s = jnp.where(keep[r0], s, jnp.finfo(F32).min)
                m_old = m_ref[h, rs]
                m_new = jnp.maximum(m_old, jnp.max(s, axis=-1, keepdims=True))
                alpha = jnp.exp2(m_old - m_new)
                p = jnp.exp2(s - m_new[:, :1])
                l_ref[h, rs] = alpha * l_ref[h, rs] + jnp.sum(p, axis=-1, keepdims=True)
                acc_ref[h, rs] = alpha * acc_ref[h, rs] + _dot(p.astype(BF16), v_ref[:nk, hs])
                m_ref[h, rs] = m_new

    @pl.when(j < i)
    def _():
        step(False)

    @pl.when(j == i)
    def _():
        step(True)
        o = jnp.concatenate([acc_ref[h] / l_ref[h] for h in range(N_HEADS)], axis=-1)
        o_ref[...] = x_ref[...] + _dot(o.astype(BF16), wo_ref[...])


def _flash_call(q, k, v, x, w_out_pad, *, tq, qs):
    nseq, s, d = x.shape
    tk = tq
    assert s % tq == 0 and tq % qs == 0
    nq = s // tq
    pairs = [(i, j) for i in range(nq) for j in range(i + 1)]
    i_tab = jnp.asarray([p[0] for p in pairs], jnp.int32)
    j_tab = jnp.asarray([p[1] for p in pairs], jnp.int32)
    kern = functools.partial(_flash_kernel, tq=tq, qs=qs)
    q_map = lambda b, t, it, jt: (b, it[t], 0)
    kv_map = lambda b, t, it, jt: (b, jt[t], 0)
    return pl.pallas_call(
        kern,
        grid_spec=pltpu.PrefetchScalarGridSpec(
            num_scalar_prefetch=2,
            grid=(nseq, len(pairs)),
            in_specs=[
                pl.BlockSpec((None, tq, QK_PAD), q_map),
                pl.BlockSpec((None, tk, QK_PAD), kv_map),
                pl.BlockSpec((None, tk, QK_PAD), kv_map),
                pl.BlockSpec((None, tq, d), q_map),
                pl.BlockSpec((QK_PAD, d), lambda b, t, it, jt: (0, 0)),
            ],
            out_specs=pl.BlockSpec((None, tq, d), q_map),
            scratch_shapes=[
                pltpu.VMEM((N_HEADS, tq, HEAD_PAD), F32),
                pltpu.VMEM((N_HEADS, tq, HEAD_PAD), F32),
                pltpu.VMEM((N_HEADS, tq, HEAD_PAD), F32),
            ],
        ),
        out_shape=jax.ShapeDtypeStruct((nseq, s, d), F32),
        compiler_params=_params("arbitrary", "arbitrary"),
        name="mla_flash",
    )(i_tab, j_tab, q, k, v, x, w_out_pad)


NEW_PAD = LANES
KEY_TILE = 1024


def _paged_kernel(pt_ref, qlat_ref, qpe_ref, cnew_ref, rnew_ref, ckv_hbm, kpe_hbm, wukr_ref,
                  gkr_ref, o_ref, cbuf, rbuf, s_ref, cbf_ref, csem, rsem, *, layer, n_pages, n_new):
    b = pl.program_id(0)
    nb = pl.num_programs(0)
    slot = b % 2
    past = n_pages * PAGE_SIZE
    n_tiles = past // KEY_TILE
    ppt = KEY_TILE // PAGE_SIZE

    def start_tile(bb, sl, i):
        for k in range(ppt):
            p = i * ppt + k
            pg = pt_ref[bb, p]
            rows = pl.ds(pl.multiple_of(p * PAGE_SIZE, PAGE_SIZE), PAGE_SIZE)
            pltpu.make_async_copy(ckv_hbm.at[layer, pg], cbuf.at[sl, rows, :], csem.at[sl]).start()
            pltpu.make_async_copy(kpe_hbm.at[layer, pg], rbuf.at[sl, i, :, pl.ds(k * PAGE_SIZE, PAGE_SIZE)],
                                  rsem.at[sl]).start()

    @pl.when(b == 0)
    def _():
        def body(i, c):
            start_tile(0, 0, i)
            return c
        lax.fori_loop(0, n_tiles, body, 0)

    pltpu.make_async_copy(cbuf.at[slot], cbuf.at[slot], csem.at[slot]).wait()
    pltpu.make_async_copy(rbuf.at[slot], rbuf.at[slot], rsem.at[slot]).wait()

    nq = n_new * N_HEADS
    wcat = jnp.concatenate([wukr_ref[...], qlat_ref[...]], axis=0)
    qpe = (qpe_ref[...] * gkr_ref[...]).astype(BF16)
    n_kn = N_HEADS * QK_NOPE

    def scores(ckv_t, kpe_tt):
        nk = ckv_t.shape[0]
        cb = ckv_t.astype(BF16)
        a = _dot_nt(wcat, cb)
        kn = a[:n_kn]
        ssq = jnp.sum((kn * kn).reshape(QK_NOPE, N_HEADS, nk), axis=0)
        ssq = ssq + jnp.sum(kpe_tt * kpe_tt, axis=0, keepdims=True)
        r = lax.rsqrt(ssq * (1.0 / QK_HEAD) + NORM_EPS)
        s = a[n_kn:] + _dot(qpe, kpe_tt.astype(BF16))
        return s * jnp.concatenate([r] * n_new, axis=0), cb

    nxt = jnp.minimum(b + 1, nb - 1)

    def tile_rows(i):
        return pl.ds(pl.multiple_of(i * KEY_TILE, KEY_TILE), KEY_TILE)

    def pass1(i, m, per_iter):
        for k in range(per_iter):
            start_tile(nxt, 1 - slot, per_iter * i + k)
        rows = tile_rows(i)
        s, cb = scores(cbuf[slot, rows, :], rbuf[slot, i])
        s_ref[i] = s
        cbf_ref[rows, :] = cb
        return jnp.maximum(m, jnp.max(s, axis=1, keepdims=True))

    per_iter = 2 if n_tiles % 2 == 0 else 1
    n_issue = n_tiles // per_iter
    m = lax.fori_loop(0, n_issue, functools.partial(pass1, per_iter=per_iter),
                      jnp.full((nq, 1), -jnp.inf, F32), unroll=2)
    m = lax.fori_loop(n_issue, n_tiles, functools.partial(pass1, per_iter=0), m, unroll=2)

    @pl.when(b == nb - 1)
    def _():
        pltpu.make_async_copy(cbuf.at[1 - slot], cbuf.at[1 - slot], csem.at[1 - slot]).wait()
        pltpu.make_async_copy(rbuf.at[1 - slot], rbuf.at[1 - slot], rsem.at[1 - slot]).wait()

    s_new, cb_new = scores(cnew_ref[...], rnew_ref[...])
    key = lax.broadcasted_iota(jnp.int32, (nq, NEW_PAD), 1)
    qt = lax.broadcasted_iota(jnp.int32, (nq, NEW_PAD), 0) // N_HEADS
    s_new = jnp.where((key < n_new) & (key <= qt), s_new, jnp.finfo(F32).min)
    m = jnp.maximum(m, jnp.max(s_new, axis=1, keepdims=True))

    p_new = jnp.exp2(s_new - m)
    l0 = jnp.sum(p_new, axis=1, keepdims=True)
    acc0 = _dot(p_new.astype(BF16), cb_new)

    def pass2(i, carry):
        l, acc = carry
        p = jnp.exp2(s_ref[i] - m)
        return l + jnp.sum(p, axis=1, keepdims=True), acc + _dot(p.astype(BF16), cbf_ref[tile_rows(i), :])

    l, o_lat = lax.fori_loop(0, n_tiles, pass2, (l0, acc0), unroll=True)

    o_ref[...] = o_lat / l


def _paged_call(page_table, qlat, qpe, ckv_new, kpe_new, cache_ckv, cache_kpe, w, *, layer, n_new):
    nb, nq, _ = qlat.shape
    n_pages = page_table.shape[1]
    past = n_pages * PAGE_SIZE
    assert past % KEY_TILE == 0 and n_new <= NEW_PAD and nq == n_new * N_HEADS
    kern = functools.partial(_paged_kernel, layer=layer, n_pages=n_pages, n_new=n_new)
    consts = [w['w_uk_r'], w['g_kr']]
    seq = lambda n1, n2: pl.BlockSpec((None, n1, n2), lambda b, pt: (b, 0, 0))
    cspec = lambda c: pl.BlockSpec(c.shape, lambda b, pt: (0,) * c.ndim)
    return pl.pallas_call(
        kern,
        grid_spec=pltpu.PrefetchScalarGridSpec(
            num_scalar_prefetch=1,
            grid=(nb,),
            in_specs=[seq(nq, KV_LORA), seq(nq, QK_ROPE), seq(NEW_PAD, KV_LORA), seq(QK_ROPE, NEW_PAD),
                      pl.BlockSpec(memory_space=pl.ANY), pl.BlockSpec(memory_space=pl.ANY)]
                     + [cspec(c) for c in consts],
            out_specs=seq(nq, KV_LORA),
            scratch_shapes=[
                pltpu.VMEM((2, past, KV_LORA), F32),
                pltpu.VMEM((2, past // KEY_TILE, QK_ROPE, KEY_TILE), F32),
                pltpu.VMEM((past // KEY_TILE, nq, KEY_TILE), F32),
                pltpu.VMEM((past, KV_LORA), BF16),
                pltpu.SemaphoreType.DMA((2,)),
                pltpu.SemaphoreType.DMA((2,)),
            ],
        ),
        out_shape=jax.ShapeDtypeStruct((nb, nq, KV_LORA), F32),
        compiler_params=_params("arbitrary"),
        name="mla_paged",
    )(page_table, qlat, qpe, ckv_new, kpe_new, cache_ckv, cache_kpe, *consts)


def _paged_weights(w):
    g_kn = w['g_kn']
    g_pad = jnp.tile(jnp.pad(g_kn[:QK_NOPE], (0, HEAD_PAD - QK_NOPE)), N_HEADS)
    w_uk_r = w['w_uk'].reshape(KV_LORA, N_HEADS, HEAD_PAD)[:, :, :QK_NOPE].transpose(2, 1, 0)
    return {
        'w_ukt': (w['w_uk'].T * g_pad[:, None]).astype(BF16),
        'w_uk_r': w_uk_r.reshape(QK_NOPE * N_HEADS, KV_LORA).astype(BF16),
        'g_kr': g_kn[QK_NOPE:].reshape(1, QK_ROPE),
        'w_uv_p': w['w_uv'].astype(BF16),
    }


def _absorb_kernel(q_ref, w_ref, o_ref):
    for h in range(N_HEADS):
        hs = slice(h * HEAD_PAD, (h + 1) * HEAD_PAD)
        o_ref[h] = _dot(q_ref[:, hs], w_ref[hs, :]).astype(o_ref.dtype)


def _absorb_call(q, w_ukt):
    rows = q.shape[0]
    return pl.pallas_call(
        _absorb_kernel,
        out_shape=jax.ShapeDtypeStruct((N_HEADS, rows, KV_LORA), BF16),
        compiler_params=pltpu.CompilerParams(vmem_limit_bytes=VMEM_LIMIT_BYTES),
        name="mla_absorb",
    )(q, w_ukt)


def _out_proj_kernel(x_ref, ol_ref, wuv_ref, wo_ref, y_ref):
    o = jnp.concatenate([_dot(ol_ref[h].astype(BF16), wuv_ref[:, h * HEAD_PAD:(h + 1) * HEAD_PAD])
                         for h in range(N_HEADS)], axis=-1)
    y_ref[...] = x_ref[...] + _dot(o.astype(BF16), wo_ref[...])


def _out_proj_call(x, o_lat, w_uv_pad, w_out_pad):
    return pl.pallas_call(
        _out_proj_kernel,
        out_shape=jax.ShapeDtypeStruct(x.shape, F32),
        compiler_params=pltpu.CompilerParams(vmem_limit_bytes=VMEM_LIMIT_BYTES),
        name="mla_out_proj",
    )(x, o_lat, w_uv_pad, w_out_pad)


def _s5_lambda(ldt, ar, ai):
    dt = jnp.exp(ldt)
    mag = jnp.exp(ar * dt)
    lr = mag * jnp.cos(ai * dt)
    li = mag * jnp.sin(ai * dt)
    den = ar * ar + ai * ai
    cr = ((lr - 1.0) * ar + li * ai) / den
    ci = (li * ar - (lr - 1.0) * ai) / den
    return lr, li, cr, ci


def _s5_disc_kernel(ldt_ref, ar_ref, ai_ref, ldte_ref, are_ref, aie_ref, br_ref, bi_ref,
                    lr_ref, li_ref, bbr_ref, bbi_ref):
    lr, li, _, _ = _s5_lambda(ldt_ref[...], ar_ref[...], ai_ref[...])
    lr_ref[...] = lr
    li_ref[...] = li
    _, _, cr, ci = _s5_lambda(ldte_ref[...], are_ref[...], aie_ref[...])
    br, bi = br_ref[...], bi_ref[...]
    bbr_ref[...] = cr * br - ci * bi
    bbi_ref[...] = cr * bi + ci * br


def _s5_disc_call(log_dt, a_re, a_im, b_re, b_im):
    g, p, c = b_re.shape
    flat = lambda a: jnp.broadcast_to(a[:, :, None], (g, p, c)).reshape(-1, LANES)
    outs = pl.pallas_call(
        _s5_disc_kernel,
        out_shape=[jax.ShapeDtypeStruct((g, p), F32)] * 2 + [jax.ShapeDtypeStruct((g * p * c // LANES, LANES), F32)] * 2,
        name="s5_discretise",
    )(log_dt, a_re, a_im, flat(log_dt), flat(a_re), flat(a_im), b_re.reshape(-1, LANES), b_im.reshape(-1, LANES))
    lr, li, bbr, bbi = outs
    return lr, li, bbr.reshape(g, p, c), bbi.reshape(g, p, c)


S5_GB = 8
S5_CW = S5_GB * SSM_GROUP
S5_SW = S5_GB * SSM_STATE
S5_LC = 1024


def _s5_kernel(x_ref, s0_ref, perm_ref, permt_ref, g_ref, win_ref, wb_ref, lam_ref, wc_ref, dsk_ref, wout_ref,
               o_ref, st_ref, u_ref, xs_ref, y_ref, *, nb, nt, nsub):
    i = pl.program_id(0)
    d = x_ref.shape[-1]
    ns = st_ref.shape[-1] // 2
    nts = nt // nsub
    rs = nts * nb

    @pl.when(i == 0)
    def _():
        st_ref[...] = s0_ref[...]

    x = x_ref[...].reshape(nb * nt, d)
    h = _rms(x, g_ref[...]).astype(BF16)
    per_sub = perm_ref.shape[0] == rs
    if per_sub:
        for sub in range(nsub):
            piece = jnp.concatenate([h[b * nt + sub * nts:b * nt + (sub + 1) * nts] for b in range(nb)], axis=0)
            u_ref[sub * rs:(sub + 1) * rs, :] = _dot(_dot(perm_ref[...], piece).astype(BF16), win_ref[...])
    else:
        u_ref[...] = _dot(_dot(perm_ref[...], h).astype(BF16), win_ref[...])

    for sub in range(nsub):
        rows = slice(sub * rs, (sub + 1) * rs)
        xs = xs_ref.at[sub % 2]
        for j in range(d // S5_CW):
            xb = _dot(u_ref[rows, j * S5_CW:(j + 1) * S5_CW].astype(BF16), wb_ref[j])
            xs[:, j * S5_SW:(j + 1) * S5_SW] = xb[:, :S5_SW]
            xs[:, ns + j * S5_SW:ns + (j + 1) * S5_SW] = xb[:, S5_SW:]

        def scan_rows(b0):
            for lc in range(0, ns, S5_LC):
                re, im = slice(lc, lc + S5_LC), slice(ns + lc, ns + lc + S5_LC)
                sr, si = st_ref[pl.ds(b0, SUBLANES), re], st_ref[pl.ds(b0, SUBLANES), im]
                for t in range(nts):
                    r8 = pl.ds(t * nb + b0, SUBLANES)
                    lr, li = lam_ref[:, re], lam_ref[:, im]
                    sr, si = lr * sr - li * si + xs[r8, re], lr * si + li * sr + xs[r8, im]
                    xs[r8, re] = sr
                    xs[r8, im] = si
                st_ref[pl.ds(b0, SUBLANES), re] = sr
                st_ref[pl.ds(b0, SUBLANES), im] = si

        if nb == SUBLANES:
            scan_rows(0)
        else:
            def scan_tile(bt, carry):
                scan_rows(pl.multiple_of(bt * SUBLANES, SUBLANES))
                return carry
            lax.fori_loop(0, nb // SUBLANES, scan_tile, 0)

        for j in range(d // S5_CW):
            cs = slice(j * S5_CW, (j + 1) * S5_CW)
            s_cat = jnp.concatenate([xs[:, j * S5_SW:(j + 1) * S5_SW],
                                     xs[:, ns + j * S5_SW:ns + (j + 1) * S5_SW]], axis=-1)
            y = _dot(s_cat.astype(BF16), wc_ref[j]) + dsk_ref[:, cs] * u_ref[rows, cs]
            y_ref[rows, cs] = jax.nn.gelu(y).astype(BF16)

    if per_sub:
        back = [_dot(permt_ref[...], y_ref[sub * rs:(sub + 1) * rs, :]).astype(BF16) for sub in range(nsub)]
        y_bm = jnp.concatenate([back[sub][b * nts:(b + 1) * nts] for b in range(nb) for sub in range(nsub)], axis=0)
    else:
        y_bm = _dot(permt_ref[...], y_ref[...]).astype(BF16)
    gl = _dot(y_bm, wout_ref[...])
    o_ref[...] = (x + gl[:, :d] * jax.nn.sigmoid(gl[:, d:])).reshape(o_ref.shape)


def _s5_call(x3, s0, w, *, nb, nt, nsub, block, n_blocks):
    d = x3.shape[-1]
    r = nb * nt
    ns2 = s0.shape[-1]
    assert block[0] * block[1] == r and nt % nsub == 0 and nb % SUBLANES == 0 and d % S5_CW == 0
    assert block[1] % SUBLANES == 0
    pt = nt // nsub if (nt // nsub) % (2 * SUBLANES) == 0 else nt
    perm = np.zeros((nb * pt, nb * pt), np.float32)
    bb, tt = np.meshgrid(np.arange(nb), np.arange(pt), indexing='ij')
    perm[(tt * nb + bb).ravel(), (bb * pt + tt).ravel()] = 1.0
    kern = functools.partial(_s5_kernel, nb=nb, nt=nt, nsub=nsub)
    consts = [s0, jnp.asarray(perm, BF16), jnp.asarray(perm.T, BF16),
              w['g'], w['w_in'], w['wb'], w['lam'], w['wc'], w['dsk'], w['w_out']]
    x_spec = pl.BlockSpec(tuple(block) + (d,), lambda i: (0, i, 0))
    return pl.pallas_call(
        kern,
        grid=(n_blocks,),
        in_specs=[x_spec] + [_const_spec(c.shape) for c in consts],
        out_specs=[x_spec, _const_spec((nb, ns2))],
        out_shape=[jax.ShapeDtypeStruct(x3.shape, F32), jax.ShapeDtypeStruct((nb, ns2), F32)],
        scratch_shapes=[
            pltpu.VMEM((r, d), F32),
            pltpu.VMEM((2, r // nsub, ns2), F32),
            pltpu.VMEM((r, d), BF16),
        ],
        compiler_params=_params("arbitrary"),
        name="s5_mixer",
    )(x3, *consts)


def _s5_weights(g, w_in, log_dt, a_re, a_im, b_re, b_im, c_re, c_im, d_skip, w_out):
    d = w_in.shape[0]
    ng, p, c = b_re.shape
    lr, li, bbr, bbi = _s5_disc_call(log_dt, a_re, a_im, b_re, b_im)
    eye = jnp.eye(S5_GB, dtype=F32)
    nblk = ng // S5_GB

    def b_tiles(bb):
        t = bb.reshape(nblk, S5_GB, p, c).transpose(0, 1, 3, 2)
        return jnp.einsum('ab,jacp->jacbp', eye, t).reshape(nblk, S5_CW, S5_SW)

    def c_tiles(cc):
        t = cc.reshape(nblk, S5_GB, c, p)
        return jnp.einsum('ba,jacp->jbpac', eye, t).reshape(nblk, S5_SW, S5_CW)

    lam = jnp.concatenate([lr.reshape(1, -1), li.reshape(1, -1)], axis=-1)
    return {
        'g': g.reshape(1, d),
        'w_in': w_in.astype(BF16),
        'wb': jnp.concatenate([b_tiles(bbr), b_tiles(bbi)], axis=-1).astype(BF16),
        'lam': jnp.broadcast_to(lam, (SUBLANES, lam.shape[-1])),
        'wc': jnp.concatenate([c_tiles(c_re), -c_tiles(c_im)], axis=1).astype(BF16),
        'dsk': d_skip.reshape(1, d),
        'w_out': w_out.astype(BF16),
    }


PROMPT_TM = 1024
FLASH_TQ = 512
FLASH_QS = 512
FFN_TM = 1024
S5_ROWS = 512
S5_NSUB = 2


def _mla_prompt_layer(x, w):
    b, s, d = x.shape
    tabs = _rope_tables(jnp.arange(s))
    q, k, v, ckv, kpe = _mla_proj_call(x, tabs, w, tm=min(PROMPT_TM, s))
    y = _flash_call(q, k, v, x, w['w_out'], tq=min(FLASH_TQ, s), qs=min(FLASH_QS, s))
    return y, ckv, kpe


def _mla_sample_layer(x, page_table, cache_ckv, cache_kpe, w, *, layer):
    b, t, d = x.shape
    past = page_table.shape[1] * PAGE_SIZE
    tabs = [jnp.tile(tb, (b, 1)) for tb in _rope_tables(past + jnp.arange(t))]
    q, _, _, ckv, kpe = _mla_proj_call(x.reshape(1, b * t, d), tabs, w, tm=b * t)
    ckv, kpe = ckv.reshape(b, t, KV_LORA), kpe.reshape(b, t, QK_ROPE)
    pad = lambda a: jnp.pad(a, ((0, 0), (0, NEW_PAD - t), (0, 0)))
    pw = _paged_weights(w)
    q = q.reshape(b * t, QK_PAD)
    nq = t * N_HEADS
    qlat = jnp.swapaxes(_absorb_call(q, pw['w_ukt']), 0, 1).reshape(b, nq, KV_LORA)
    qpe = q.reshape(b, t, N_HEADS, HEAD_PAD)[..., QK_NOPE:QK_HEAD].astype(F32).reshape(b, nq, QK_ROPE)
    o_lat = _paged_call(page_table, qlat, qpe, pad(ckv), jnp.swapaxes(pad(kpe), 1, 2),
                        cache_ckv, jnp.swapaxes(cache_kpe, 2, 3), pw, layer=layer, n_new=t)
    o_lat = jnp.moveaxis(o_lat.reshape(b * t, N_HEADS, KV_LORA), 1, 0)
    y = _out_proj_call(x.reshape(b * t, d), o_lat, pw['w_uv_p'], w['w_out'])
    return y.reshape(b, t, d), ckv, kpe


def _s5_layer(x, s0_re, s0_im, w):
    b, t, d = x.shape
    nt = min(S5_ROWS // b, t)
    s0 = jnp.concatenate([s0_re.reshape(b, -1), s0_im.reshape(b, -1)], axis=-1)
    if nt % SUBLANES == 0:
        y, st = _s5_call(x, s0, w, nb=b, nt=nt, nsub=S5_NSUB, block=(b, nt), n_blocks=t // nt)
    else:
        assert nt == t
        x3 = x.reshape(b * t // SUBLANES, SUBLANES, d)
        y, st = _s5_call(x3, s0, w, nb=b, nt=nt, nsub=S5_NSUB, block=x3.shape[:2], n_blocks=1)
        y = y.reshape(b, t, d)
    ns = st.shape[-1] // 2
    return y, st[:, :ns].reshape(s0_re.shape), st[:, ns:].reshape(s0_im.shape)


def _ffn_prompt_layer(x, w):
    b, s, d = x.shape
    zero = jnp.zeros((b, CONV_W - 1, w[1].shape[-1]), F32)
    return _ffn_call(x, zero, *w, nb=1, tm=min(FFN_TM, s))


def _ffn_sample_layer(x, prev, w):
    b, t, d = x.shape
    x_tm = jnp.swapaxes(x, 0, 1).reshape(1, t * b, d)
    prev_tm = jnp.swapaxes(prev, 0, 1).reshape(1, (CONV_W - 1) * b, prev.shape[-1])
    y, st = _ffn_call(x_tm, prev_tm, *w, nb=b, tm=t * b)
    return jnp.swapaxes(y.reshape(t, b, d), 0, 1), jnp.swapaxes(st.reshape(CONV_W - 1, b, -1), 0, 1)


def kernel(x_prompt, x_sample, cache_ckv, cache_kpe, state_ssm_re, state_ssm_im, state_conv, page_table,
           attn_norm, attn_w_in, attn_q_lat_norm, attn_w_q_up, attn_kv_lat_norm, attn_w_kv_up,
           attn_q_norm, attn_k_norm, attn_w_out,
           ssm_norm, ssm_w_in, ssm_log_dt, ssm_a_re, ssm_a_im, ssm_b_re, ssm_b_im,
           ssm_c_re, ssm_c_im, ssm_d, ssm_w_out,
           ffn_norm, ffn_w_up, ffn_conv_w, ffn_conv_b, ffn_w_down):
    xp, xs = x_prompt, x_sample
    depth = ffn_w_up.shape[0]
    bp = xp.shape[0]
    ssm_zero = jnp.zeros((bp,) + state_ssm_re.shape[2:], F32)
    ckv_p, kpe_p, ckv_s, kpe_s = [], [], [], []
    sre_p, sim_p, sre_s, sim_s = [], [], [], []
    conv_p, conv_s = [], []
    for i in range(depth):
        j = i // 2
        if i % 2 == 0:
            w = _mla_weights(attn_norm[j], attn_w_in[j], attn_q_lat_norm[j], attn_w_q_up[j], attn_kv_lat_norm[j],
                             attn_w_kv_up[j], attn_q_norm[j], attn_k_norm[j], attn_w_out[j])
            xp, c_p, r_p = _mla_prompt_layer(xp, w)
            xs, c_s, r_s = _mla_sample_layer(xs, page_table, cache_ckv, cache_kpe, w, layer=j)
            ckv_p.append(c_p)
            kpe_p.append(r_p)
            ckv_s.append(c_s)
            kpe_s.append(r_s)
        else:
            w = _s5_weights(ssm_norm[j], ssm_w_in[j], ssm_log_dt[j], ssm_a_re[j], ssm_a_im[j], ssm_b_re[j],
                            ssm_b_im[j], ssm_c_re[j], ssm_c_im[j], ssm_d[j], ssm_w_out[j])
            xp, r_p, i_p = _s5_layer(xp, ssm_zero, ssm_zero, w)
            xs, r_s, i_s = _s5_layer(xs, state_ssm_re[j], state_ssm_im[j], w)
            sre_p.append(r_p)
            sim_p.append(i_p)
            sre_s.append(r_s)
            sim_s.append(i_s)
        fw = (ffn_norm[i], ffn_w_up[i].astype(BF16), ffn_conv_w[i], ffn_conv_b[i], ffn_w_down[i].astype(BF16))
        xp, cv_p = _ffn_prompt_layer(xp, fw)
        xs, cv_s = _ffn_sample_layer(xs, state_conv[i], fw)
        conv_p.append(cv_p)
        conv_s.append(cv_s)
    return (xp, xs,
            jnp.stack(ckv_p), jnp.stack(kpe_p), jnp.stack(sre_p), jnp.stack(sim_p), jnp.stack(conv_p),
            jnp.stack(ckv_s), jnp.stack(kpe_s), jnp.stack(sre_s), jnp.stack(sim_s), jnp.stack(conv_s))
```

```python
import functools
import math

import jax
import jax.numpy as jnp
import numpy as np
from jax import lax
from jax.experimental import pallas as pl
from jax.experimental.pallas import tpu as pltpu

LANES = 128
SUBLANES = 8
VMEM_LIMIT_BYTES = 56 * 1024 * 1024

N_HEADS = 8
QK_NOPE = 64
QK_ROPE = 32
QK_HEAD = QK_NOPE + QK_ROPE
V_HEAD = 64
HEAD_PAD = LANES
QK_PAD = N_HEADS * HEAD_PAD
Q_LORA = 384
KV_LORA = 256
PAGE_SIZE = 128
ROPE_THETA = 10000.0
SSM_GROUP = 16
SSM_STATE = 64
CONV_W = 3
NORM_EPS = 1e-6

BF16 = jnp.bfloat16
F32 = jnp.float32


def _params(*sem):
    return pltpu.CompilerParams(dimension_semantics=sem, vmem_limit_bytes=VMEM_LIMIT_BYTES)


def _dot(a, b):
    return jnp.dot(a, b, preferred_element_type=F32)


def _dot_nt(a, b):
    return lax.dot_general(a, b, (((1,), (1,)), ((), ())), preferred_element_type=F32)


def _dot_tn(a, b):
    return lax.dot_general(a, b, (((0,), (0,)), ((), ())), preferred_element_type=F32)


def _rms(x, g):
    return x * lax.rsqrt(jnp.mean(x * x, axis=-1, keepdims=True) + NORM_EPS) * g


def _const_spec(shape, single=False):
    n = len(shape)
    if single:
        return pl.BlockSpec(shape, lambda *_: (0,) * n, pipeline_mode=pl.Buffered(1))
    return pl.BlockSpec(shape, lambda *_: (0,) * n)


FFN_CHUNK = 256

def _ffn_kernel(x_ref, prev_ref, g_ref, wup_ref, cw_ref, cb_ref, wdn_ref, o_ref, st_ref, h_ref, uc_ref, a_ref,
                *, nb, tm, off, d_ff, nc):
    hist = (CONV_W - 1) * nb

    @pl.when(pl.program_id(1) == 0)
    def _():
        st_ref[...] = prev_ref[...]

    x = x_ref[...]
    h_ref[...] = _rms(x, g_ref[...]).astype(BF16)

    for ci, c0 in enumerate(range(0, d_ff, nc)):
        def conv(half, lo):
            buf = uc_ref.at[2 * (ci % 2) + half]
            u = _dot(h_ref[...], wup_ref[:, lo:lo + nc])
            buf[off - hist:off, :] = st_ref[:, lo:lo + nc]
            buf[off:off + tm, :] = u
            st_ref[:, lo:lo + nc] = u[tm - hist:, :]
            c = cb_ref[:, lo:lo + nc]
            for j in range(CONV_W - 1):
                r0 = off - (CONV_W - 1 - j) * nb
                c = c + cw_ref[j:j + 1, lo:lo + nc] * buf[r0:r0 + tm, :]
            return c + cw_ref[CONV_W - 1:CONV_W, lo:lo + nc] * u
        val = conv(0, c0)
        gate = conv(1, d_ff + c0)
        a_ref[:, c0:c0 + nc] = (jax.nn.silu(gate) * val).astype(BF16)

    o_ref[...] = x + _dot(a_ref[...], wdn_ref[...])


def _ffn_call(x, prev, g, w_up, conv_w, conv_b, w_down, *, nb, tm):
    nseq, rows, d = x.shape
    d_ff = w_down.shape[0]
    hist = (CONV_W - 1) * nb
    off = -(-hist // SUBLANES) * SUBLANES
    nc = FFN_CHUNK
    assert rows % tm == 0 and tm % nb == 0 and d_ff % nc == 0 and tm >= hist
    kern = functools.partial(_ffn_kernel, nb=nb, tm=tm, off=off, d_ff=d_ff, nc=nc)
    return pl.pallas_call(
        kern,
        grid=(nseq, rows // tm),
        in_specs=[
            pl.BlockSpec((None, tm, d), lambda s, i: (s, i, 0)),
            pl.BlockSpec((None, hist, 2 * d_ff), lambda s, i: (s, 0, 0)),
            _const_spec((1, d)),
            _const_spec((d, 2 * d_ff), single=True),
            _const_spec((CONV_W, 2 * d_ff)),
            _const_spec((1, 2 * d_ff)),
            _const_spec((d_ff, d), single=True),
        ],
        out_specs=[
            pl.BlockSpec((None, tm, d), lambda s, i: (s, i, 0)),
            pl.BlockSpec((None, hist, 2 * d_ff), lambda s, i: (s, 0, 0)),
        ],
        out_shape=[
            jax.ShapeDtypeStruct((nseq, rows, d), F32),
            jax.ShapeDtypeStruct((nseq, hist, 2 * d_ff), F32),
        ],
        scratch_shapes=[
            pltpu.VMEM((tm, d), BF16),
            pltpu.VMEM((4, off + tm, nc), F32),
            pltpu.VMEM((tm, d_ff), BF16),
        ],
        compiler_params=_params("arbitrary", "arbitrary"),
        name="conv_ffn",
    )(x, prev, g.reshape(1, d), w_up, conv_w, conv_b.reshape(1, 2 * d_ff), w_down)


def _rope128(x, c, sa, sb):
    n = x.shape[-1]
    return x * c + pltpu.roll(x, n - QK_ROPE // 2, 1) * sa + pltpu.roll(x, QK_ROPE // 2, 1) * sb


def _head_ms(x, ones2):
    sq = (x * x).astype(BF16)
    parts = [_dot(sq[:, c0:c0 + 2 * HEAD_PAD], ones2) for c0 in range(0, x.shape[-1], 2 * HEAD_PAD)]
    return jnp.concatenate(parts, axis=-1) * (1.0 / QK_HEAD)


def _mla_proj_kernel(x_ref, tc_ref, tsa_ref, tsb_ref, gin_ref, win_ref, gql_ref, wq_ref, gkvl_ref, wkv_ref,
                     gq_ref, gk_ref, ones2_ref,
                     q_ref, k_ref, v_ref, ckv_ref, kpe_ref):
    x = x_ref[...]
    h = _rms(x, gin_ref[...]).astype(BF16)
    proj = _dot(h, win_ref[...])
    cq = proj[:, :Q_LORA]
    ckv = proj[:, Q_LORA:Q_LORA + KV_LORA]
    kp = proj[:, Q_LORA + KV_LORA:]
    tc, tsa, tsb = tc_ref[...], tsa_ref[...], tsb_ref[...]
    ones2 = ones2_ref[...]

    q = _dot(_rms(cq, gql_ref[...]).astype(BF16), wq_ref[...])
    rep = lambda t: jnp.concatenate([t] * N_HEADS, axis=-1)
    q = _rope128(q, rep(tc), rep(tsa), rep(tsb))
    q = q * lax.rsqrt(_head_ms(q, ones2) + NORM_EPS) * gq_ref[...]
    q_ref[...] = q.astype(q_ref.dtype)

    ckv_n = _rms(ckv, gkvl_ref[...])
    ckv_ref[...] = ckv_n
    kp = _rope128(kp, tc, tsa, tsb)
    kpe_ref[...] = kp[:, QK_NOPE:QK_HEAD]

    kv = _dot(ckv_n.astype(BF16), wkv_ref[...])
    k = kv[:, :QK_PAD] + rep(kp)
    k = k * lax.rsqrt(_head_ms(k, ones2) + NORM_EPS) * gk_ref[...]
    k_ref[...] = k.astype(k_ref.dtype)
    v_ref[...] = kv[:, QK_PAD:].astype(v_ref.dtype)


def _mla_proj_call(x, tabs, w, *, tm):
    nseq, rows, d = x.shape
    assert rows % tm == 0
    row_spec = lambda n: pl.BlockSpec((None, tm, n), lambda s, i: (s, i, 0))
    tab_spec = pl.BlockSpec((tm, LANES), lambda s, i: (i, 0))
    consts = [w['g_in'], w['w_in'], w['g_ql'], w['w_q'], w['g_kvl'], w['w_kv'], w['g_q'], w['g_k'], w['ones2']]
    return pl.pallas_call(
        _mla_proj_kernel,
        grid=(nseq, rows // tm),
        in_specs=[row_spec(d), tab_spec, tab_spec, tab_spec] + [_const_spec(c.shape) for c in consts],
        out_specs=[row_spec(QK_PAD), row_spec(QK_PAD), row_spec(QK_PAD), row_spec(KV_LORA), row_spec(QK_ROPE)],
        out_shape=[
            jax.ShapeDtypeStruct((nseq, rows, QK_PAD), BF16),
            jax.ShapeDtypeStruct((nseq, rows, QK_PAD), BF16),
            jax.ShapeDtypeStruct((nseq, rows, QK_PAD), BF16),
            jax.ShapeDtypeStruct((nseq, rows, KV_LORA), F32),
            jax.ShapeDtypeStruct((nseq, rows, QK_ROPE), F32),
        ],
        compiler_params=_params("arbitrary", "arbitrary"),
        name="mla_proj",
    )(x, *tabs, *consts)


def _head_pad_cols(w, per_head, lo, hi):
    k = w.shape[0]
    w = w.reshape(k, N_HEADS, per_head)[:, :, lo:hi]
    w = jnp.pad(w, ((0, 0), (0, 0), (0, HEAD_PAD - (hi - lo))))
    return w.reshape(k, QK_PAD)


def _head_pad_vec(g):
    return jnp.tile(jnp.pad(g, (0, HEAD_PAD - g.shape[0])), N_HEADS).reshape(1, QK_PAD)


def _rope_tables(pos):
    half = QK_ROPE // 2
    inv = ROPE_THETA ** (-jnp.arange(half, dtype=F32) * (2.0 / QK_ROPE))
    ang = pos.astype(F32)[:, None] * inv[None, :]
    cos, sin = jnp.cos(ang), jnp.sin(ang)
    t = pos.shape[0]
    z = lambda n: jnp.zeros((t, n), F32)
    c = jnp.concatenate([jnp.ones((t, QK_NOPE), F32), cos, cos, z(HEAD_PAD - QK_HEAD)], axis=-1)
    sa = jnp.concatenate([z(QK_NOPE), -sin, z(half), z(HEAD_PAD - QK_HEAD)], axis=-1)
    sb = jnp.concatenate([z(QK_NOPE), z(half), sin, z(HEAD_PAD - QK_HEAD)], axis=-1)
    return c, sa, sb


def _mla_weights(g_in, w_in, g_ql, w_q_up, g_kvl, w_kv_up, g_qn, g_kn, w_out):
    d = w_in.shape[0]
    w_kpe = jnp.pad(w_in[:, Q_LORA + KV_LORA:], ((0, 0), (QK_NOPE, HEAD_PAD - QK_HEAD)))
    w_in_ext = jnp.concatenate([w_in[:, :Q_LORA + KV_LORA], w_kpe], axis=1)
    w_uk = _head_pad_cols(w_kv_up, QK_NOPE + V_HEAD, 0, QK_NOPE)
    w_uv = _head_pad_cols(w_kv_up, QK_NOPE + V_HEAD, QK_NOPE, QK_NOPE + V_HEAD)
    blk = np.kron(np.eye(2, dtype=np.float32), np.ones((HEAD_PAD, HEAD_PAD), np.float32))
    w_out_pad = jnp.pad(w_out.reshape(N_HEADS, V_HEAD, d), ((0, 0), (0, HEAD_PAD - V_HEAD), (0, 0)))
    return {
        'g_in': g_in.reshape(1, d),
        'w_in': w_in_ext.astype(BF16),
        'g_ql': g_ql.reshape(1, Q_LORA),
        'w_q': _head_pad_cols(w_q_up, QK_HEAD, 0, QK_HEAD).astype(BF16),
        'g_kvl': g_kvl.reshape(1, KV_LORA),
        'w_kv': jnp.concatenate([w_uk, w_uv], axis=1).astype(BF16),
        'g_q': _head_pad_vec(g_qn) * (QK_HEAD ** -0.5 * math.log2(math.e)),
        'g_k': _head_pad_vec(g_kn),
        'ones2': jnp.asarray(blk, BF16),
        'w_out': w_out_pad.reshape(QK_PAD, d).astype(BF16),
        'w_uk': w_uk, 'w_uv': w_uv, 'g_kn': g_kn,
    }


def _flash_kernel(it_ref, jt_ref, q_ref, k_ref, v_ref, x_ref, wo_ref, o_ref, m_ref, l_ref, acc_ref, *, tq, qs):
    tk = tq
    i = it_ref[pl.program_id(1)]
    j = jt_ref[pl.program_id(1)]

    @pl.when(j == 0)
    def _():
        m_ref[...] = jnp.full(m_ref.shape, -jnp.inf, F32)
        l_ref[...] = jnp.zeros(l_ref.shape, F32)
        acc_ref[...] = jnp.zeros(acc_ref.shape, F32)

    def step(diagonal):
        keep = {}
        if diagonal:
            for r0 in range(0, tq, qs):
                row = r0 + lax.broadcasted_iota(jnp.int32, (qs, r0 + qs), 0)
                keep[r0] = lax.broadcasted_iota(jnp.int32, (qs, r0 + qs), 1) <= row
        for h in range(N_HEADS):
            hs = slice(h * HEAD_PAD, (h + 1) * HEAD_PAD)
            for r0 in range(0, tq, qs):
                rs = slice(r0, r0 + qs)
                nk = r0 + qs if diagonal else tk
                s = _dot_nt(q_ref[rs, hs], k_ref[:nk, hs])
                if diagonal:
                    s = jnp.where(keep[r0], s, jnp.finfo(F32).min)
                m_old = m_ref[h, rs]
                m_new = jnp.maximum(m_old, jnp.max(s, axis=-1, keepdims=True))
                alpha = jnp.exp2(m_old - m_new)
                p = jnp.exp2(s - m_new[:, :1])
                l_ref[h, rs] = alpha * l_ref[h, rs] + jnp.sum(p, axis=-1, keepdims=True)
                acc_ref[h, rs] = alpha * acc_ref[h, rs] + _dot(p.astype(BF16), v_ref[:nk, hs])
                m_ref[h, rs] = m_new

    @pl.when(j < i)
    def _():
        step(False)

    @pl.when(j == i)
    def _():
        step(True)
        o = jnp.concatenate([acc_ref[h] / l_ref[h] for h in range(N_HEADS)], axis=-1)
        o_ref[...] = x_ref[...] + _dot(o.astype(BF16), wo_ref[...])


def _flash_call(q, k, v, x, w_out_pad, *, tq, qs):
    nseq, s, d = x.shape
    tk = tq
    assert s % tq == 0 and tq % qs == 0
    nq = s // tq
    pairs = [(i, j) for i in range(nq) for j in range(i + 1)]
    i_tab = jnp.asarray([p[0] for p in pairs], jnp.int32)
    j_tab = jnp.asarray([p[1] for p in pairs], jnp.int32)
    kern = functools.partial(_flash_kernel, tq=tq, qs=qs)
    q_map = lambda b, t, it, jt: (b, it[t], 0)
    kv_map = lambda b, t, it, jt: (b, jt[t], 0)
    return pl.pallas_call(
        kern,
        grid_spec=pltpu.PrefetchScalarGridSpec(
            num_scalar_prefetch=2,
            grid=(nseq, len(pairs)),
            in_specs=[
                pl.BlockSpec((None, tq, QK_PAD), q_map),
                pl.BlockSpec((None, tk, QK_PAD), kv_map),
                pl.BlockSpec((None, tk, QK_PAD), kv_map),
                pl.BlockSpec((None, tq, d), q_map),
                pl.BlockSpec((QK_PAD, d), lambda b, t, it, jt: (0, 0)),
            ],
            out_specs=pl.BlockSpec((None, tq, d), q_map),
            scratch_shapes=[
                pltpu.VMEM((N_HEADS, tq, HEAD_PAD), F32),
                pltpu.VMEM((N_HEADS, tq, HEAD_PAD), F32),
                pltpu.VMEM((N_HEADS, tq, HEAD_PAD), F32),
            ],
        ),
        out_shape=jax.ShapeDtypeStruct((nseq, s, d), F32),
        compiler_params=_params("arbitrary", "arbitrary"),
        name="mla_flash",
    )(i_tab, j_tab, q, k, v, x, w_out_pad)


NEW_PAD = LANES
KEY_TILE = 1024


def _paged_kernel(pt_ref, qlat_ref, qpe_ref, cnew_ref, rnew_ref, ckv_hbm, kpe_hbm, wukr_ref,
                  gkr_ref, o_ref, cbuf, rbuf, s_ref, cbf_ref, csem, rsem, *, layer, n_pages, n_new):
    b = pl.program_id(0)
    nb = pl.num_programs(0)
    slot = b % 2
    past = n_pages * PAGE_SIZE
    n_tiles = past // KEY_TILE
    ppt = KEY_TILE // PAGE_SIZE

    def start_tile(bb, sl, i):
        for k in range(ppt):
            p = i * ppt + k
            pg = pt_ref[bb, p]
            rows = pl.ds(pl.multiple_of(p * PAGE_SIZE, PAGE_SIZE), PAGE_SIZE)
            pltpu.make_async_copy(ckv_hbm.at[layer, pg], cbuf.at[sl, rows, :], csem.at[sl]).start(priority=k % 2)
            pltpu.make_async_copy(kpe_hbm.at[layer, pg], rbuf.at[sl, i, :, pl.ds(k * PAGE_SIZE, PAGE_SIZE)],
                                  rsem.at[sl]).start(priority=(k + 1) % 2)

    @pl.when(b == 0)
    def _():
        def body(i, c):
            start_tile(0, 0, i)
            return c
        lax.fori_loop(0, n_tiles, body, 0)

    pltpu.make_async_copy(cbuf.at[slot], cbuf.at[slot], csem.at[slot]).wait()
    pltpu.make_async_copy(rbuf.at[slot], rbuf.at[slot], rsem.at[slot]).wait()

    nq = n_new * N_HEADS
    wcat = jnp.concatenate([wukr_ref[...], qlat_ref[...]], axis=0)
    qpe = (qpe_ref[...] * gkr_ref[...]).astype(BF16)
    n_kn = N_HEADS * QK_NOPE

    def scores(ckv_t, kpe_tt):
        nk = ckv_t.shape[0]
        cb = ckv_t.astype(BF16)
        a = _dot_nt(wcat, cb)
        kn = a[:n_kn]
        ssq = jnp.sum((kn * kn).reshape(QK_NOPE, N_HEADS, nk), axis=0)
        ssq = ssq + jnp.sum(kpe_tt * kpe_tt, axis=0, keepdims=True)
        r = lax.rsqrt(ssq * (1.0 / QK_HEAD) + NORM_EPS)
        s = a[n_kn:] + _dot(qpe, kpe_tt.astype(BF16))
        return s * jnp.concatenate([r] * n_new, axis=0), cb

    nxt = jnp.minimum(b + 1, nb - 1)

    def tile_rows(i):
        return pl.ds(pl.multiple_of(i * KEY_TILE, KEY_TILE), KEY_TILE)

    def pass1(i, m, per_iter):
        for k in range(per_iter):
            start_tile(nxt, 1 - slot, per_iter * i + k)
        rows = tile_rows(i)
        s, cb = scores(cbuf[slot, rows, :], rbuf[slot, i])
        s_ref[i] = s
        cbf_ref[rows, :] = cb
        return jnp.maximum(m, jnp.max(s, axis=1, keepdims=True))

    per_iter = 2 if n_tiles % 2 == 0 else 1
    n_issue = n_tiles // per_iter
    m = lax.fori_loop(0, n_issue, functools.partial(pass1, per_iter=per_iter),
                      jnp.full((nq, 1), -jnp.inf, F32), unroll=2)
    m = lax.fori_loop(n_issue, n_tiles, functools.partial(pass1, per_iter=0), m, unroll=2)

    @pl.when(b == nb - 1)
    def _():
        pltpu.make_async_copy(cbuf.at[1 - slot], cbuf.at[1 - slot], csem.at[1 - slot]).wait()
        pltpu.make_async_copy(rbuf.at[1 - slot], rbuf.at[1 - slot], rsem.at[1 - slot]).wait()

    s_new, cb_new = scores(cnew_ref[...], rnew_ref[...])
    key = lax.broadcasted_iota(jnp.int32, (nq, NEW_PAD), 1)
    qt = lax.broadcasted_iota(jnp.int32, (nq, NEW_PAD), 0) // N_HEADS
    s_new = jnp.where((key < n_new) & (key <= qt), s_new, jnp.finfo(F32).min)
    m = jnp.maximum(m, jnp.max(s_new, axis=1, keepdims=True))

    p_new = jnp.exp2(s_new - m)
    l0 = jnp.sum(p_new, axis=1, keepdims=True)
    acc0 = _dot(p_new.astype(BF16), cb_new)

    def pass2(i, carry):
        l, acc = carry
        p = jnp.exp2(s_ref[i] - m)
        return l + jnp.sum(p, axis=1, keepdims=True), acc + _dot(p.astype(BF16), cbf_ref[tile_rows(i), :])

    l, o_lat = lax.fori_loop(0, n_tiles, pass2, (l0, acc0), unroll=True)

    o_ref[...] = o_lat / l


def _paged_call(page_table, qlat, qpe, ckv_new, kpe_new, cache_ckv, cache_kpe, w, *, layer, n_new):
    nb, nq, _ = qlat.shape
    n_pages = page_table.shape[1]
    past = n_pages * PAGE_SIZE
    assert past % KEY_TILE == 0 and n_new <= NEW_PAD and nq == n_new * N_HEADS
    kern = functools.partial(_paged_kernel, layer=layer, n_pages=n_pages, n_new=n_new)
    consts = [w['w_uk_r'], w['g_kr']]
    seq = lambda n1, n2: pl.BlockSpec((None, n1, n2), lambda b, pt: (b, 0, 0))
    cspec = lambda c: pl.BlockSpec(c.shape, lambda b, pt: (0,) * c.ndim)
    return pl.pallas_call(
        kern,
        grid_spec=pltpu.PrefetchScalarGridSpec(
            num_scalar_prefetch=1,
            grid=(nb,),
            in_specs=[seq(nq, KV_LORA), seq(nq, QK_ROPE), seq(NEW_PAD, KV_LORA), seq(QK_ROPE, NEW_PAD),
                      pl.BlockSpec(memory_space=pl.ANY), pl.BlockSpec(memory_space=pl.ANY)]
                     + [cspec(c) for c in consts],
            out_specs=seq(nq, KV_LORA),
            scratch_shapes=[
                pltpu.VMEM((2, past, KV_LORA), F32),
                pltpu.VMEM((2, past // KEY_TILE, QK_ROPE, KEY_TILE), F32),
                pltpu.VMEM((past // KEY_TILE, nq, KEY_TILE), F32),
                pltpu.VMEM((past, KV_LORA), BF16),
                pltpu.SemaphoreType.DMA((2,)),
                pltpu.SemaphoreType.DMA((2,)),
            ],
        ),
        out_shape=jax.ShapeDtypeStruct((nb, nq, KV_LORA), F32),
        compiler_params=_params("arbitrary"),
        name="mla_paged",
    )(page_table, qlat, qpe, ckv_new, kpe_new, cache_ckv, cache_kpe, *consts)


def _paged_weights(w):
    g_kn = w['g_kn']
    g_pad = jnp.tile(jnp.pad(g_kn[:QK_NOPE], (0, HEAD_PAD - QK_NOPE)), N_HEADS)
    w_uk_r = w['w_uk'].reshape(KV_LORA, N_HEADS, HEAD_PAD)[:, :, :QK_NOPE].transpose(2, 1, 0)
    return {
        'w_ukt': (w['w_uk'].T * g_pad[:, None]).astype(BF16),
        'w_uk_r': w_uk_r.reshape(QK_NOPE * N_HEADS, KV_LORA).astype(BF16),
        'g_kr': g_kn[QK_NOPE:].reshape(1, QK_ROPE),
        'w_uv_p': w['w_uv'].astype(BF16),
    }


def _absorb_kernel(q_ref, w_ref, o_ref):
    for h in range(N_HEADS):
        hs = slice(h * HEAD_PAD, (h + 1) * HEAD_PAD)
        o_ref[h] = _dot(q_ref[:, hs], w_ref[hs, :]).astype(o_ref.dtype)


def _absorb_call(q, w_ukt):
    rows = q.shape[0]
    return pl.pallas_call(
        _absorb_kernel,
        out_shape=jax.ShapeDtypeStruct((N_HEADS, rows, KV_LORA), BF16),
        compiler_params=pltpu.CompilerParams(vmem_limit_bytes=VMEM_LIMIT_BYTES),
        name="mla_absorb",
    )(q, w_ukt)


def _out_proj_kernel(x_ref, ol_ref, wuv_ref, wo_ref, y_ref):
    o = jnp.concatenate([_dot(ol_ref[h].astype(BF16), wuv_ref[:, h * HEAD_PAD:(h + 1) * HEAD_PAD])
                         for h in range(N_HEADS)], axis=-1)
    y_ref[...] = x_ref[...] + _dot(o.astype(BF16), wo_ref[...])


def _out_proj_call(x, o_lat, w_uv_pad, w_out_pad):
    return pl.pallas_call(
        _out_proj_kernel,
        out_shape=jax.ShapeDtypeStruct(x.shape, F32),
        compiler_params=pltpu.CompilerParams(vmem_limit_bytes=VMEM_LIMIT_BYTES),
        name="mla_out_proj",
    )(x, o_lat, w_uv_pad, w_out_pad)


def _s5_lambda(ldt, ar, ai):
    dt = jnp.exp(ldt)
    mag = jnp.exp(ar * dt)
    lr = mag * jnp.cos(ai * dt)
    li = mag * jnp.sin(ai * dt)
    den = ar * ar + ai * ai
    cr = ((lr - 1.0) * ar + li * ai) / den
    ci = (li * ar - (lr - 1.0) * ai) / den
    return lr, li, cr, ci


def _s5_disc_kernel(ldt_ref, ar_ref, ai_ref, ldte_ref, are_ref, aie_ref, br_ref, bi_ref,
                    lr_ref, li_ref, bbr_ref, bbi_ref):
    lr, li, _, _ = _s5_lambda(ldt_ref[...], ar_ref[...], ai_ref[...])
    lr_ref[...] = lr
    li_ref[...] = li
    _, _, cr, ci = _s5_lambda(ldte_ref[...], are_ref[...], aie_ref[...])
    br, bi = br_ref[...], bi_ref[...]
    bbr_ref[...] = cr * br - ci * bi
    bbi_ref[...] = cr * bi + ci * br


def _s5_disc_call(log_dt, a_re, a_im, b_re, b_im):
    g, p, c = b_re.shape
    flat = lambda a: jnp.broadcast_to(a[:, :, None], (g, p, c)).reshape(-1, LANES)
    outs = pl.pallas_call(
        _s5_disc_kernel,
        out_shape=[jax.ShapeDtypeStruct((g, p), F32)] * 2 + [jax.ShapeDtypeStruct((g * p * c // LANES, LANES), F32)] * 2,
        name="s5_discretise",
    )(log_dt, a_re, a_im, flat(log_dt), flat(a_re), flat(a_im), b_re.reshape(-1, LANES), b_im.reshape(-1, LANES))
    lr, li, bbr, bbi = outs
    return lr, li, bbr.reshape(g, p, c), bbi.reshape(g, p, c)


S5_GB = 8
S5_CW = S5_GB * SSM_GROUP
S5_SW = S5_GB * SSM_STATE
S5_LC = 1024


def _s5_kernel(x_ref, s0_ref, perm_ref, permt_ref, g_ref, win_ref, wb_ref, lam_ref, wc_ref, dsk_ref, wout_ref,
               o_ref, st_ref, u_ref, xs_ref, y_ref, *, nb, nt, nsub):
    i = pl.program_id(0)
    d = x_ref.shape[-1]
    ns = st_ref.shape[-1] // 2
    nts = nt // nsub
    rs = nts * nb

    @pl.when(i == 0)
    def _():
        st_ref[...] = s0_ref[...]

    x = x_ref[...].reshape(nb * nt, d)
    h = _rms(x, g_ref[...]).astype(BF16)
    per_sub = perm_ref.shape[0] == rs
    if per_sub:
        for sub in range(nsub):
            piece = jnp.concatenate([h[b * nt + sub * nts:b * nt + (sub + 1) * nts] for b in range(nb)], axis=0)
            u_ref[sub * rs:(sub + 1) * rs, :] = _dot(_dot(perm_ref[...], piece).astype(BF16), win_ref[...])
    else:
        u_ref[...] = _dot(_dot(perm_ref[...], h).astype(BF16), win_ref[...])

    for sub in range(nsub):
        rows = slice(sub * rs, (sub + 1) * rs)
        xs = xs_ref.at[sub % 2]
        for j in range(d // S5_CW):
            xb = _dot(u_ref[rows, j * S5_CW:(j + 1) * S5_CW].astype(BF16), wb_ref[j])
            xs[:, j * S5_SW:(j + 1) * S5_SW] = xb[:, :S5_SW]
            xs[:, ns + j * S5_SW:ns + (j + 1) * S5_SW] = xb[:, S5_SW:]

        def scan_rows(b0):
            for lc in range(0, ns, S5_LC):
                re, im = slice(lc, lc + S5_LC), slice(ns + lc, ns + lc + S5_LC)
                sr, si = st_ref[pl.ds(b0, SUBLANES), re], st_ref[pl.ds(b0, SUBLANES), im]
                for t in range(nts):
                    r8 = pl.ds(t * nb + b0, SUBLANES)
                    lr, li = lam_ref[:, re], lam_ref[:, im]
                    sr, si = lr * sr - li * si + xs[r8, re], lr * si + li * sr + xs[r8, im]
                    xs[r8, re] = sr
                    xs[r8, im] = si
                st_ref[pl.ds(b0, SUBLANES), re] = sr
                st_ref[pl.ds(b0, SUBLANES), im] = si

        if nb == SUBLANES:
            scan_rows(0)
        else:
            def scan_tile(bt, carry):
                scan_rows(pl.multiple_of(bt * SUBLANES, SUBLANES))
                return carry
            lax.fori_loop(0, nb // SUBLANES, scan_tile, 0)

        for j in range(d // S5_CW):
            cs = slice(j * S5_CW, (j + 1) * S5_CW)
            s_cat = jnp.concatenate([xs[:, j * S5_SW:(j + 1) * S5_SW],
                                     xs[:, ns + j * S5_SW:ns + (j + 1) * S5_SW]], axis=-1)
            y = _dot(s_cat.astype(BF16), wc_ref[j]) + dsk_ref[:, cs] * u_ref[rows, cs]
            y_ref[rows, cs] = jax.nn.gelu(y).astype(BF16)

    if per_sub:
        back = [_dot(permt_ref[...], y_ref[sub * rs:(sub + 1) * rs, :]).astype(BF16) for sub in range(nsub)]
        y_bm = jnp.concatenate([back[sub][b * nts:(b + 1) * nts] for b in range(nb) for sub in range(nsub)], axis=0)
    else:
        y_bm = _dot(permt_ref[...], y_ref[...]).astype(BF16)
    gl = _dot(y_bm, wout_ref[...])
    o_ref[...] = (x + gl[:, :d] * jax.nn.sigmoid(gl[:, d:])).reshape(o_ref.shape)


def _s5_call(x3, s0, w, *, nb, nt, nsub, block, n_blocks):
    d = x3.shape[-1]
    r = nb * nt
    ns2 = s0.shape[-1]
    assert block[0] * block[1] == r and nt % nsub == 0 and nb % SUBLANES == 0 and d % S5_CW == 0
    assert block[1] % SUBLANES == 0
    pt = nt // nsub if (nt // nsub) % (2 * SUBLANES) == 0 else nt
    perm = np.zeros((nb * pt, nb * pt), np.float32)
    bb, tt = np.meshgrid(np.arange(nb), np.arange(pt), indexing='ij')
    perm[(tt * nb + bb).ravel(), (bb * pt + tt).ravel()] = 1.0
    kern = functools.partial(_s5_kernel, nb=nb, nt=nt, nsub=nsub)
    consts = [s0, jnp.asarray(perm, BF16), jnp.asarray(perm.T, BF16),
              w['g'], w['w_in'], w['wb'], w['lam'], w['wc'], w['dsk'], w['w_out']]
    x_spec = pl.BlockSpec(tuple(block) + (d,), lambda i: (0, i, 0))
    return pl.pallas_call(
        kern,
        grid=(n_blocks,),
        in_specs=[x_spec] + [_const_spec(c.shape) for c in consts],
        out_specs=[x_spec, _const_spec((nb, ns2))],
        out_shape=[jax.ShapeDtypeStruct(x3.shape, F32), jax.ShapeDtypeStruct((nb, ns2), F32)],
        scratch_shapes=[
            pltpu.VMEM((r, d), F32),
            pltpu.VMEM((2, r // nsub, ns2), F32),
            pltpu.VMEM((r, d), BF16),
        ],
        compiler_params=_params("arbitrary"),
        name="s5_mixer",
    )(x3, *consts)


def _s5_weights(g, w_in, log_dt, a_re, a_im, b_re, b_im, c_re, c_im, d_skip, w_out):
    d = w_in.shape[0]
    ng, p, c = b_re.shape
    lr, li, bbr, bbi = _s5_disc_call(log_dt, a_re, a_im, b_re, b_im)
    eye = jnp.eye(S5_GB, dtype=F32)
    nblk = ng // S5_GB

    def b_tiles(bb):
        t = bb.reshape(nblk, S5_GB, p, c).transpose(0, 1, 3, 2)
        return jnp.einsum('ab,jacp->jacbp', eye, t).reshape(nblk, S5_CW, S5_SW)

    def c_tiles(cc):
        t = cc.reshape(nblk, S5_GB, c, p)
        return jnp.einsum('ba,jacp->jbpac', eye, t).reshape(nblk, S5_SW, S5_CW)

    lam = jnp.concatenate([lr.reshape(1, -1), li.reshape(1, -1)], axis=-1)
    return {
        'g': g.reshape(1, d),
        'w_in': w_in.astype(BF16),
        'wb': jnp.concatenate([b_tiles(bbr), b_tiles(bbi)], axis=-1).astype(BF16),
        'lam': jnp.broadcast_to(lam, (SUBLANES, lam.shape[-1])),
        'wc': jnp.concatenate([c_tiles(c_re), -c_tiles(c_im)], axis=1).astype(BF16),
        'dsk': d_skip.reshape(1, d),
        'w_out': w_out.astype(BF16),
    }


PROMPT_TM = 1024
FLASH_TQ = 512
FLASH_QS = 512
FFN_TM = 1024
S5_ROWS = 512
S5_NSUB = 2


def _mla_prompt_layer(x, w):
    b, s, d = x.shape
    tabs = _rope_tables(jnp.arange(s))
    q, k, v, ckv, kpe = _mla_proj_call(x, tabs, w, tm=min(PROMPT_TM, s))
    y = _flash_call(q, k, v, x, w['w_out'], tq=min(FLASH_TQ, s), qs=min(FLASH_QS, s))
    return y, ckv, kpe


def _mla_sample_layer(x, page_table, cache_ckv, cache_kpe, w, *, layer):
    b, t, d = x.shape
    past = page_table.shape[1] * PAGE_SIZE
    tabs = [jnp.tile(tb, (b, 1)) for tb in _rope_tables(past + jnp.arange(t))]
    q, _, _, ckv, kpe = _mla_proj_call(x.reshape(1, b * t, d), tabs, w, tm=b * t)
    ckv, kpe = ckv.reshape(b, t, KV_LORA), kpe.reshape(b, t, QK_ROPE)
    pad = lambda a: jnp.pad(a, ((0, 0), (0, NEW_PAD - t), (0, 0)))
    pw = _paged_weights(w)
    q = q.reshape(b * t, QK_PAD)
    nq = t * N_HEADS
    qlat = jnp.swapaxes(_absorb_call(q, pw['w_ukt']), 0, 1).reshape(b, nq, KV_LORA)
    qpe = q.reshape(b, t, N_HEADS, HEAD_PAD)[..., QK_NOPE:QK_HEAD].astype(F32).reshape(b, nq, QK_ROPE)
    o_lat = _paged_call(page_table, qlat, qpe, pad(ckv), jnp.swapaxes(pad(kpe), 1, 2),
                        cache_ckv, jnp.swapaxes(cache_kpe, 2, 3), pw, layer=layer, n_new=t)
    o_lat = jnp.moveaxis(o_lat.reshape(b * t, N_HEADS, KV_LORA), 1, 0)
    y = _out_proj_call(x.reshape(b * t, d), o_lat, pw['w_uv_p'], w['w_out'])
    return y.reshape(b, t, d), ckv, kpe


def _s5_layer(x, s0_re, s0_im, w):
    b, t, d = x.shape
    nt = min(S5_ROWS // b, t)
    s0 = jnp.concatenate([s0_re.reshape(b, -1), s0_im.reshape(b, -1)], axis=-1)
    if nt % SUBLANES == 0:
        y, st = _s5_call(x, s0, w, nb=b, nt=nt, nsub=S5_NSUB, block=(b, nt), n_blocks=t // nt)
    else:
        assert nt == t
        x3 = x.reshape(b * t // SUBLANES, SUBLANES, d)
        y, st = _s5_call(x3, s0, w, nb=b, nt=nt, nsub=S5_NSUB, block=x3.shape[:2], n_blocks=1)
        y = y.reshape(b, t, d)
    ns = st.shape[-1] // 2
    return y, st[:, :ns].reshape(s0_re.shape), st[:, ns:].reshape(s0_im.shape)


def _ffn_prompt_layer(x, w):
    b, s, d = x.shape
    zero = jnp.zeros((b, CONV_W - 1, w[1].shape[-1]), F32)
    return _ffn_call(x, zero, *w, nb=1, tm=min(FFN_TM, s))


def _ffn_sample_layer(x, prev, w):
    b, t, d = x.shape
    x_tm = jnp.swapaxes(x, 0, 1).reshape(1, t * b, d)
    prev_tm = jnp.swapaxes(prev, 0, 1).reshape(1, (CONV_W - 1) * b, prev.shape[-1])
    y, st = _ffn_call(x_tm, prev_tm, *w, nb=b, tm=t * b)
    return jnp.swapaxes(y.reshape(t, b, d), 0, 1), jnp.swapaxes(st.reshape(CONV_W - 1, b, -1), 0, 1)


def kernel(x_prompt, x_sample, cache_ckv, cache_kpe, state_ssm_re, state_ssm_im, state_conv, page_table,
           attn_norm, attn_w_in, attn_q_lat_norm, attn_w_q_up, attn_kv_lat_norm, attn_w_kv_up,
           attn_q_norm, attn_k_norm, attn_w_out,
           ssm_norm, ssm_w_in, ssm_log_dt, ssm_a_re, ssm_a_im, ssm_b_re, ssm_b_im,
           ssm_c_re, ssm_c_im, ssm_d, ssm_w_out,
           ffn_norm, ffn_w_up, ffn_conv_w, ffn_conv_b, ffn_w_down):
    xp, xs = x_prompt, x_sample
    depth = ffn_w_up.shape[0]
    bp = xp.shape[0]
    ssm_zero = jnp.zeros((bp,) + state_ssm_re.shape[2:], F32)
    ckv_p, kpe_p, ckv_s, kpe_s = [], [], [], []
    sre_p, sim_p, sre_s, sim_s = [], [], [], []
    conv_p, conv_s = [], []
    for i in range(depth):
        j = i // 2
        if i % 2 == 0:
            w = _mla_weights(attn_norm[j], attn_w_in[j], attn_q_lat_norm[j], attn_w_q_up[j], attn_kv_lat_norm[j],
                             attn_w_kv_up[j], attn_q_norm[j], attn_k_norm[j], attn_w_out[j])
            xp, c_p, r_p = _mla_prompt_layer(xp, w)
            xs, c_s, r_s = _mla_sample_layer(xs, page_table, cache_ckv, cache_kpe, w, layer=j)
            ckv_p.append(c_p)
            kpe_p.append(r_p)
            ckv_s.append(c_s)
            kpe_s.append(r_s)
        else:
            w = _s5_weights(ssm_norm[j], ssm_w_in[j], ssm_log_dt[j], ssm_a_re[j], ssm_a_im[j], ssm_b_re[j],
                            ssm_b_im[j], ssm_c_re[j], ssm_c_im[j], ssm_d[j], ssm_w_out[j])
            xp, r_p, i_p = _s5_layer(xp, ssm_zero, ssm_zero, w)
            xs, r_s, i_s = _s5_layer(xs, state_ssm_re[j], state_ssm_im[j], w)
            sre_p.append(r_p)
            sim_p.append(i_p)
            sre_s.append(r_s)
            sim_s.append(i_s)
        fw = (ffn_norm[i], ffn_w_up[i].astype(BF16), ffn_conv_w[i], ffn_conv_b[i], ffn_w_down[i].astype(BF16))
        xp, cv_p = _ffn_prompt_layer(xp, fw)
        xs, cv_s = _ffn_sample_layer(xs, state_conv[i], fw)
        conv_p.append(cv_p)
        conv_s.append(cv_s)
    return (xp, xs,
            jnp.stack(ckv_p), jnp.stack(kpe_p), jnp.stack(sre_p), jnp.stack(sim_p), jnp.stack(conv_p),
            jnp.stack(ckv_s), jnp.stack(kpe_s), jnp.stack(sre_s), jnp.stack(sim_s), jnp.stack(conv_s))
```
